```python
import jax, jax.numpy as jnp
from jax import lax
import numpy as np

D_MODEL = 1024
BATCH = 8
SEQ = 8192
DEPTH = 4

HEAD_DIM = 64
N_HEADS_SB = 4
N_HEADS_FOX = 4
DIL_PATTERNS = ((128, 1), (512, 4), (2048, 16))
N_HEADS_PER_DIL = 4
N_HEADS_DIL = N_HEADS_PER_DIL * len(DIL_PATTERNS)
N_HEADS_MEM = 4
N_MEM = 256
N_BRANCH = 4
Q_BLOCK = 128
ROPE_THETA = 10000.0
EPS = 1e-6
NEG_BIG = -1e4

W_SB = N_HEADS_SB * HEAD_DIM
W_FOX = N_HEADS_FOX * HEAD_DIM
W_DILQKV = N_HEADS_DIL * HEAD_DIM
W_DIL = N_HEADS_PER_DIL * HEAD_DIM
W_MEM = N_HEADS_MEM * HEAD_DIM
IN_SPLITS = (W_SB,) * 4 + (W_FOX,) * 4 + (N_HEADS_FOX,) + (W_DILQKV,) * 3 + (W_DIL, W_MEM, W_MEM, N_BRANCH * D_MODEL)
D_IN = 4 * W_SB + 4 * W_FOX + N_HEADS_FOX + 3 * W_DILQKV + W_DIL + 2 * W_MEM + N_BRANCH * D_MODEL

kernel_name = 'hybrid_sb_fox_dilated_mem_gated'

F32 = jnp.float32


def rms_norm(x, g):
    x32 = x.astype(F32)
    y = x32 * lax.rsqrt(jnp.mean(x32 * x32, axis=-1, keepdims=True) + EPS)
    return (y * g.astype(F32)).astype(x.dtype)


def heads(t, n):
    return t.reshape(t.shape[:-1] + (n, HEAD_DIM))


def rope(x):
    S = x.shape[1]
    half = HEAD_DIM // 2
    inv = ROPE_THETA ** (-jnp.arange(half, dtype=F32) / half)
    ang = jnp.arange(S, dtype=F32)[:, None] * inv[None, :]
    cos = jnp.cos(ang)[None, :, None, :]
    sin = jnp.sin(ang)[None, :, None, :]
    x32 = x.astype(F32)
    x1, x2 = x32[..., :half], x32[..., half:]
    return jnp.concatenate([x1 * cos - x2 * sin, x2 * cos + x1 * sin], axis=-1).astype(x.dtype)


def stick_breaking_attention(q, k, v):
    B, S, H, _ = q.shape
    q = q * HEAD_DIM ** -0.5
    nb = S // Q_BLOCK
    outs = []
    for i in range(nb):
        n = i + 1
        qb = q[:, i * Q_BLOCK:(i + 1) * Q_BLOCK]
        kb, vb = k[:, :n * Q_BLOCK], v[:, :n * Q_BLOCK]
        z = jnp.einsum('bqhd,bkhd->bhqk', qb, kb, preferred_element_type=F32)
        qpos = i * Q_BLOCK + jnp.arange(Q_BLOCK)
        valid = jnp.arange(n * Q_BLOCK)[None, :] < qpos[:, None]
        z = jnp.where(valid, z, NEG_BIG).reshape(B, H, Q_BLOCK, n, Q_BLOCK)
        log_keep = jax.nn.log_sigmoid(-z)
        r_local = lax.cumsum(log_keep, axis=4, reverse=True)
        totals = r_local[..., 0]
        r_later = lax.cumsum(totals, axis=3, reverse=True) - totals
        a = jnp.exp(z + r_local + r_later[..., None]).reshape(B, H, Q_BLOCK, n * Q_BLOCK)
        outs.append(jnp.einsum('bhqk,bkhd->bqhd', a.astype(v.dtype), vb))
    return jnp.concatenate(outs, axis=1)


def forgetting_attention(q, k, v, log_f):
    B, S, H, _ = q.shape
    q = q * HEAD_DIM ** -0.5
    nb = S // Q_BLOCK
    Fk = jnp.transpose(lax.cumsum(log_f, axis=1), (0, 2, 1))[:, :, None, :]
    outs = []
    for i in range(nb):
        n = i + 1
        qb = q[:, i * Q_BLOCK:(i + 1) * Q_BLOCK]
        kb, vb = k[:, :n * Q_BLOCK], v[:, :n * Q_BLOCK]
        s = jnp.einsum('bqhd,bkhd->bhqk', qb, kb, preferred_element_type=F32) - Fk[..., :n * Q_BLOCK]
        qpos = i * Q_BLOCK + jnp.arange(Q_BLOCK)
        valid = jnp.arange(n * Q_BLOCK)[None, :] <= qpos[:, None]
        p = jax.nn.softmax(jnp.where(valid, s, -jnp.inf), axis=-1)
        outs.append(jnp.einsum('bhqk,bkhd->bqhd', p.astype(v.dtype), vb))
    return jnp.concatenate(outs, axis=1)


def dilated_window_attention(q, k, v, window, dilation):
    B, S, H, _ = q.shape
    scale = HEAD_DIM ** -0.5
    n_back = window // dilation
    L = S // dilation
    nb = -(-L // Q_BLOCK)
    Lp = nb * Q_BLOCK
    Bd = B * dilation

    def strided(t):
        t = t.reshape(B, L, dilation, H, HEAD_DIM).transpose(0, 2, 1, 3, 4)
        return t.reshape(Bd, L, H, HEAD_DIM)

    def band(t):
        t = jnp.pad(strided(t), ((0, 0), (Q_BLOCK, Lp - L), (0, 0), (0, 0)))
        t = t.reshape(Bd, nb + 1, Q_BLOCK, H, HEAD_DIM)
        return jnp.concatenate([t[:, :-1], t[:, 1:]], axis=2)

    qs = jnp.pad(strided(q), ((0, 0), (0, Lp - L), (0, 0), (0, 0))).reshape(Bd, nb, Q_BLOCK, H, HEAD_DIM)
    kb, vb = band(k), band(v)
    s = jnp.einsum('bnqhd,bnkhd->bnhqk', qs, kb, preferred_element_type=F32) * scale
    i = jnp.arange(Q_BLOCK)[:, None]
    c = jnp.arange(2 * Q_BLOCK)[None, :]
    dist = i + Q_BLOCK - c
    u_key = (jnp.arange(nb) * Q_BLOCK - Q_BLOCK)[:, None, None] + c[None]
    valid = (dist >= 0) & (dist <= n_back) & (u_key >= 0)
    s = jnp.where(valid[None, :, None], s, -jnp.inf)
    m = jnp.max(s, axis=-1, keepdims=True)
    p = jnp.exp(s - m)
    denom = jnp.sum(p, axis=-1, keepdims=True)
    o = jnp.einsum('bnhqk,bnkhd->bnqhd', (p / denom).astype(v.dtype), vb)
    lse = (m + jnp.log(denom))[..., 0]

    def unstrided(t):
        t = t[:, :L].reshape((B, dilation, L) + t.shape[2:])
        return jnp.swapaxes(t, 1, 2).reshape((B, S) + t.shape[3:])

    o = unstrided(o.reshape(Bd, Lp, H, HEAD_DIM))
    lse = unstrided(jnp.swapaxes(lse, 2, 3).reshape(Bd, Lp, H))
    return o, lse


def dilated_mixture(q, k, v):
    outs, lses = [], []
    for g, (window, dilation) in enumerate(DIL_PATTERNS):
        sl = slice(g * N_HEADS_PER_DIL, (g + 1) * N_HEADS_PER_DIL)
        o, lse = dilated_window_attention(q[:, :, sl], k[:, :, sl], v[:, :, sl], window, dilation)
        outs.append(o)
        lses.append(lse)
    w = jax.nn.softmax(jnp.stack(lses), axis=0)
    o = jnp.sum(w[..., None] * jnp.stack(outs).astype(F32), axis=0)
    return o.astype(q.dtype)


def memory_cross_attention(q, km, vm):
    s = jnp.einsum('bqhd,bmhd->bhqm', q, km, preferred_element_type=F32) * HEAD_DIM ** -0.5
    p = jax.nn.softmax(s, axis=-1)
    return jnp.einsum('bhqm,bmhd->bqhd', p.astype(vm.dtype), vm)


def hybrid_layer(x, mem, ln_g, mem_ln_g, qk_g, w_in, b_forget, w_mem_kv,
                 w_br_sb, w_br_fox, w_br_dil, w_br_mem, w_out):
    B, S, _ = x.shape
    h = rms_norm(x, ln_g)
    u = jnp.einsum('bsd,de->bse', h, w_in)
    split_idx = np.cumsum(IN_SPLITS)[:-1].tolist()
    (q_sb, k_sb, v_sb, z_sb, q_fx, k_fx, v_fx, z_fx, f_fx,
     q_dl, k_dl, v_dl, z_dl, q_mm, z_mm, gate_logits) = jnp.split(u, split_idx, axis=-1)

    o_sb = stick_breaking_attention(heads(q_sb, N_HEADS_SB), heads(k_sb, N_HEADS_SB),
                                    heads(v_sb, N_HEADS_SB)).reshape(B, S, W_SB)

    log_f = jax.nn.log_sigmoid((f_fx + b_forget).astype(F32))
    o_fx = forgetting_attention(rms_norm(heads(q_fx, N_HEADS_FOX), qk_g[0]),
                                rms_norm(heads(k_fx, N_HEADS_FOX), qk_g[1]),
                                heads(v_fx, N_HEADS_FOX), log_f).reshape(B, S, W_FOX)

    o_dl = dilated_mixture(rope(rms_norm(heads(q_dl, N_HEADS_DIL), qk_g[2])),
                           rope(rms_norm(heads(k_dl, N_HEADS_DIL), qk_g[3])),
                           heads(v_dl, N_HEADS_DIL)).reshape(B, S, W_DIL)

    mkv = jnp.einsum('bmd,de->bme', rms_norm(mem, mem_ln_g), w_mem_kv)
    km, vm = jnp.split(mkv, 2, axis=-1)
    o_mm = memory_cross_attention(rms_norm(heads(q_mm, N_HEADS_MEM), qk_g[4]),
                                  rms_norm(heads(km, N_HEADS_MEM), qk_g[5]),
                                  heads(vm, N_HEADS_MEM)).reshape(B, S, W_MEM)

    y_sb = jnp.einsum('bsc,cd->bsd', o_sb * jax.nn.silu(z_sb), w_br_sb)
    y_fx = jnp.einsum('bsc,cd->bsd', o_fx * jax.nn.silu(z_fx), w_br_fox)
    y_dl = jnp.einsum('bsc,cd->bsd', o_dl * jax.nn.silu(z_dl), w_br_dil)
    y_mm = jnp.einsum('bsc,cd->bsd', o_mm * jax.nn.silu(z_mm), w_br_mem)

    g = jax.nn.sigmoid(gate_logits.astype(F32)).astype(x.dtype).reshape(B, S, N_BRANCH, D_MODEL)
    merged = g[:, :, 0] * y_sb + g[:, :, 1] * y_fx + g[:, :, 2] * y_dl + g[:, :, 3] * y_mm
    return x + jnp.einsum('bsd,de->bse', merged, w_out)


def setup_inputs(seed: int = 0) -> dict:
    key = jax.random.key(seed)
    ks = jax.random.split(key, 13)

    def nrm(k, shape, fan_in):
        return jax.random.normal(k, shape, F32) * fan_in ** -0.5

    return {
        'x': jax.random.normal(ks[0], (BATCH, SEQ, D_MODEL), F32),
        'mem': jax.random.normal(ks[1], (BATCH, N_MEM, D_MODEL), F32),
        'ln_gain': 1.0 + 0.02 * jax.random.normal(ks[2], (DEPTH, D_MODEL), F32),
        'mem_ln_gain': 1.0 + 0.02 * jax.random.normal(ks[3], (DEPTH, D_MODEL), F32),
        'qk_gain': 1.0 + 0.02 * jax.random.normal(ks[4], (DEPTH, 6, HEAD_DIM), F32),
        'w_in': nrm(ks[5], (DEPTH, D_MODEL, D_IN), D_MODEL),
        'b_forget': jax.random.uniform(ks[6], (DEPTH, N_HEADS_FOX), F32, 2.0, 6.0),
        'w_mem_kv': nrm(ks[7], (DEPTH, D_MODEL, 2 * W_MEM), D_MODEL),
        'w_br_sb': nrm(ks[8], (DEPTH, W_SB, D_MODEL), W_SB),
        'w_br_fox': nrm(ks[9], (DEPTH, W_FOX, D_MODEL), W_FOX),
        'w_br_dil': nrm(ks[10], (DEPTH, W_DIL, D_MODEL), W_DIL),
        'w_br_mem': nrm(ks[11], (DEPTH, W_MEM, D_MODEL), W_MEM),
        'w_out': nrm(ks[12], (DEPTH, D_MODEL, D_MODEL), D_MODEL),
    }


def reference(x, mem, ln_gain, mem_ln_gain, qk_gain, w_in, b_forget, w_mem_kv,
              w_br_sb, w_br_fox, w_br_dil, w_br_mem, w_out):
    for l in range(DEPTH):
        x = hybrid_layer(x, mem, ln_gain[l], mem_ln_gain[l], qk_gain[l], w_in[l], b_forget[l],
                         w_mem_kv[l], w_br_sb[l], w_br_fox[l], w_br_dil[l], w_br_mem[l], w_out[l])
    return x
```

```python
import functools

import jax
import jax.numpy as jnp
import numpy as np
from jax import lax
from jax.experimental import pallas as pl
from jax.experimental.pallas import tpu as pltpu

F32 = jnp.float32
BF16 = jnp.bfloat16

D_MODEL = 1024
HEAD_DIM = 64
HEADS_PER_BRANCH = 4
BRANCH_WIDTH = HEADS_PER_BRANCH * HEAD_DIM
DIL_PATTERNS = ((128, 1), (512, 4), (2048, 16))
N_DIL = len(DIL_PATTERNS)
N_BRANCH = 4
DIL_BLOCK = 128
ROPE_THETA = 10000.0
EPS = 1e-6
MASKED_LOGIT = -1e4
QK_SCALE = HEAD_DIM ** -0.5
SB_UNDERFLOW = -105.0
V7X_VMEM_LIMIT_BYTES = 56 * 1024 * 1024
FORGET_ROWS = 8

_NT = (((1,), (1,)), ((), ()))


def _dot(a, b):
    return jnp.dot(a, b, preferred_element_type=F32)


def _dot_nt(a, b):
    return lax.dot_general(a, b, _NT, preferred_element_type=F32)


def _split_dot(x, m, terms):
    out = None
    r = x
    for t in range(terms):
        p = r.astype(BF16)
        d = _dot(p, m)
        out = d if out is None else out + d
        if t + 1 < terms:
            r = r - p.astype(F32)
    return out


def _rms_rows(x, gain):
    ms = jnp.mean(x * x, axis=-1, keepdims=True)
    return x * lax.rsqrt(ms + EPS) * gain


def _head_rms(x, group_mean, gain):
    ms = _split_dot(x * x, group_mean, 2)
    return x * lax.rsqrt(ms + EPS) * gain


def _sigmoid(x):
    return 1.0 / (1.0 + jnp.exp(-x))


def _log_sigmoid(x):
    return jnp.minimum(x, 0.0) - jnp.log(1.0 + jnp.exp(-jnp.abs(x)))


def _rope(x, cos, sin_signed):
    lane = lax.broadcasted_iota(jnp.int32, x.shape, 1)
    first_half = (lane & (HEAD_DIM - 1)) < (HEAD_DIM // 2)
    w = x.shape[1]
    partner = jnp.where(first_half, pltpu.roll(x, w - HEAD_DIM // 2, 1), pltpu.roll(x, HEAD_DIM // 2, 1))
    return x * cos + partner * sin_signed


def _mem_kv_kernel(mem_ref, g_ref, w_ref, gk_ref, gm_ref, km_ref, vm_ref):
    h = _rms_rows(mem_ref[0], g_ref[0]).astype(BF16)
    kv = _dot(h, w_ref[0])
    km_ref[0, 0] = _head_rms(kv[:, :BRANCH_WIDTH], gm_ref[...], gk_ref[0]).astype(BF16)
    vm_ref[0, 0] = kv[:, BRANCH_WIDTH:].astype(BF16)


def _mem_kv(mem, mem_ln_gain, w_mem_kv, gain_k, group_mean):
    depth = w_mem_kv.shape[0]
    b, n_mem, _ = mem.shape
    out = jax.ShapeDtypeStruct((depth, b, n_mem, BRANCH_WIDTH), BF16)
    return pl.pallas_call(
        _mem_kv_kernel,
        grid=(depth, b),
        in_specs=[
            pl.BlockSpec((1, n_mem, D_MODEL), lambda l, i: (i, 0, 0)),
            pl.BlockSpec((1, 1, D_MODEL), lambda l, i: (l, 0, 0)),
            pl.BlockSpec((1, D_MODEL, 2 * BRANCH_WIDTH), lambda l, i: (l, 0, 0)),
            pl.BlockSpec((1, 1, BRANCH_WIDTH), lambda l, i: (l, 0, 0)),
            pl.BlockSpec((BRANCH_WIDTH, BRANCH_WIDTH), lambda l, i: (0, 0)),
        ],
        out_specs=[pl.BlockSpec((1, 1, n_mem, BRANCH_WIDTH), lambda l, i: (l, i, 0, 0))] * 2,
        out_shape=[out, out],
        compiler_params=pltpu.CompilerParams(dimension_semantics=("arbitrary", "arbitrary")),
        name="mem_kv",
    )(mem, mem_ln_gain, w_mem_kv, gain_k, group_mean)


_SEC_Q_SB, _SEC_K_SB, _SEC_V_SB = 0, 1, 2
_SEC_Q_FX, _SEC_K_FX, _SEC_V_FX = 3, 4, 5
_SEC_Q_DL, _SEC_K_DL, _SEC_V_DL = 6, 9, 12
_SEC_Q_MM = 15
_SEC_Z = 16
_N_SEC = 20


def _proj_kernel(x_ref, g_ref, w_ref, wf_ref, bf_ref, gains_ref, cos_ref, sin_ref, gm_ref, tri_ref,
                 q_sb_ref, k_sb_ref, v_sb_ref, q_fx_ref, k_fx_ref, v_fx_ref,
                 q_dl_ref, k_dl_ref, v_dl_ref, q_mm_ref, z_ref, fcum_ref, carry_ref):
    h = _rms_rows(x_ref[0], g_ref[...]).astype(BF16)
    gm = gm_ref[...]

    def sec(i):
        return _dot(h, w_ref[:, i * BRANCH_WIDTH:(i + 1) * BRANCH_WIDTH])

    def gain(i):
        return gains_ref[i:i + 1, :]

    q_sb_ref[0] = (sec(_SEC_Q_SB) * QK_SCALE).astype(BF16)
    k_sb_ref[0] = sec(_SEC_K_SB).astype(BF16)
    v_sb_ref[0] = sec(_SEC_V_SB).astype(BF16)

    q_fx_ref[0] = (_head_rms(sec(_SEC_Q_FX), gm, gain(0)) * QK_SCALE).astype(BF16)
    k_fx_ref[0] = _head_rms(sec(_SEC_K_FX), gm, gain(1)).astype(BF16)
    v_fx_ref[0] = sec(_SEC_V_FX).astype(BF16)

    cos = cos_ref[...]
    sin = sin_ref[...]
    for c in range(N_DIL):
        cols = slice(c * BRANCH_WIDTH, (c + 1) * BRANCH_WIDTH)
        q = _rope(_head_rms(sec(_SEC_Q_DL + c), gm, gain(2)), cos, sin)
        q_dl_ref[0, :, cols] = (q * QK_SCALE).astype(BF16)
        k = _rope(_head_rms(sec(_SEC_K_DL + c), gm, gain(3)), cos, sin)
        k_dl_ref[0, :, cols] = k.astype(BF16)
        v_dl_ref[0, :, cols] = sec(_SEC_V_DL + c).astype(BF16)

    q_mm_ref[0] = (_head_rms(sec(_SEC_Q_MM), gm, gain(4)) * QK_SCALE).astype(BF16)

    for c in range(N_BRANCH):
        z = sec(_SEC_Z + c)
        z_ref[0, :, c * BRANCH_WIDTH:(c + 1) * BRANCH_WIDTH] = z * _sigmoid(z)

    @pl.when(pl.program_id(1) == 0)
    def _():
        carry_ref[...] = jnp.zeros_like(carry_ref)

    log_f = _log_sigmoid(_dot_nt(wf_ref[...], h) + bf_ref[...])
    fcum = _split_dot(log_f, tri_ref[...], 3) + carry_ref[...]
    fcum_ref[0] = fcum
    carry_ref[...] = fcum[:, -1:]


def _proj(x, ln_g, w_main, wf_t, bf, gains, cos_t, sin_t, group_mean, tri, tm):
    b, s, _ = x.shape
    bw = BRANCH_WIDTH

    def tok(width, dtype):
        return jax.ShapeDtypeStruct((b, s, width), dtype)

    def tok_spec(width):
        return pl.BlockSpec((1, tm, width), lambda i, j: (i, j, 0))

    def const_spec(shape):
        return pl.BlockSpec(shape, lambda i, j: (0,) * len(shape))

    out_shape = ([tok(bw, BF16)] * 6 + [tok(N_DIL * bw, BF16)] * 3 + [tok(bw, BF16), tok(N_BRANCH * bw, F32),
                 jax.ShapeDtypeStruct((b, FORGET_ROWS, s), F32)])
    out_specs = ([tok_spec(bw)] * 6 + [tok_spec(N_DIL * bw)] * 3 + [tok_spec(bw), tok_spec(N_BRANCH * bw),
                 pl.BlockSpec((1, FORGET_ROWS, tm), lambda i, j: (i, 0, j))])
    return pl.pallas_call(
        _proj_kernel,
        grid=(b, s // tm),
        in_specs=[
            tok_spec(D_MODEL),
            const_spec((1, D_MODEL)),
            const_spec((D_MODEL, _N_SEC * bw)),
            const_spec((FORGET_ROWS, D_MODEL)),
            const_spec((FORGET_ROWS, 1)),
            const_spec((8, bw)),
            pl.BlockSpec((tm, bw), lambda i, j: (j, 0)),
            pl.BlockSpec((tm, bw), lambda i, j: (j, 0)),
            const_spec((bw, bw)),
            const_spec((tm, tm)),
        ],
        out_specs=out_specs,
        out_shape=out_shape,
        scratch_shapes=[pltpu.VMEM((FORGET_ROWS, 1), F32)],
        compiler_params=pltpu.CompilerParams(dimension_semantics=("arbitrary", "arbitrary"),
                                             vmem_limit_bytes=V7X_VMEM_LIMIT_BYTES),
        name="proj",
    )(x, ln_g, w_main, wf_t, bf, gains, cos_t, sin_t, group_mean, tri)


def _sb_kernel(q_ref, k_ref, v_ref, tri_ref, o_ref, acc_ref, c_ref, *, tq):
    i = pl.program_id(1)
    q = q_ref[0]
    tri = tri_ref[...]
    acc_ref[...] = jnp.zeros_like(acc_ref)
    c_ref[...] = jnp.zeros_like(c_ref)
    row = lax.broadcasted_iota(jnp.int32, (tq, tq), 0)
    col = lax.broadcasted_iota(jnp.int32, (tq, tq), 1)
    delta = row - col

    def cond(state):
        j, c_max = state
        return jnp.logical_and(j >= 0, c_max > SB_UNDERFLOW)

    def body(state):
        j, _ = state
        k0 = pl.multiple_of(j * tq, tq)
        kb = k_ref[0, pl.ds(k0, tq), :]
        vb = v_ref[0, pl.ds(k0, tq), :]
        valid = delta > (j - i) * tq
        c_max = jnp.float32(-jnp.inf)
        for hd in range(HEADS_PER_BRANCH):
            cols = slice(hd * HEAD_DIM, (hd + 1) * HEAD_DIM)
            z = jnp.where(valid, _dot_nt(q[:, cols], kb[:, cols]), MASKED_LOGIT)
            t = jnp.log(1.0 + jnp.exp(-jnp.abs(z)))
            log_beta = jnp.minimum(z, 0.0) - t
            log_keep = -jnp.maximum(z, 0.0) - t
            c = c_ref[hd]
            later = _dot(log_keep.astype(BF16), tri) + c
            a = jnp.exp(log_beta + later)
            acc_ref[:, cols] += _dot(a.astype(BF16), vb[:, cols])
            c = c + jnp.sum(log_keep, axis=-1, keepdims=True)
            c_ref[hd] = c
            c_max = jnp.maximum(c_max, jnp.max(c))
        return j - 1, c_max

    lax.while_loop(cond, body, (i, jnp.float32(0.0)))
    o_ref[0] = acc_ref[...]


def _sb_attention(q, k, v, tri_strict, tq):
    b, s, w = q.shape
    return pl.pallas_call(
        functools.partial(_sb_kernel, tq=tq),
        grid=(b, s // tq),
        in_specs=[
            pl.BlockSpec((1, tq, w), lambda i, j: (i, j, 0)),
            pl.BlockSpec((1, s, w), lambda i, j: (i, 0, 0)),
            pl.BlockSpec((1, s, w), lambda i, j: (i, 0, 0)),
            pl.BlockSpec((tq, tq), lambda i, j: (0, 0)),
        ],
        out_specs=pl.BlockSpec((1, tq, w), lambda i, j: (i, j, 0)),
        out_shape=jax.ShapeDtypeStruct((b, s, w), F32),
        scratch_shapes=[pltpu.VMEM((tq, w), F32), pltpu.VMEM((HEADS_PER_BRANCH, tq, 1), F32)],
        compiler_params=pltpu.CompilerParams(dimension_semantics=("arbitrary", "arbitrary"),
                                             vmem_limit_bytes=V7X_VMEM_LIMIT_BYTES),
        name="sb_attention",
    )(q, k, v, tri_strict)


def _fox_kernel(q_ref, k_ref, v_ref, f_ref, o_ref, acc_ref, m_ref, l_ref, *, tq):
    i = pl.program_id(1)
    q = q_ref[0]
    acc_ref[...] = jnp.zeros_like(acc_ref)
    m_ref[...] = jnp.full_like(m_ref, -jnp.inf)
    l_ref[...] = jnp.zeros_like(l_ref)
    row = lax.broadcasted_iota(jnp.int32, (tq, tq), 0)
    col = lax.broadcasted_iota(jnp.int32, (tq, tq), 1)
    delta = row - col

    def body(j, carry):
        k0 = pl.multiple_of(j * tq, tq)
        kb = k_ref[0, pl.ds(k0, tq), :]
        vb = v_ref[0, pl.ds(k0, tq), :]
        valid = delta >= (j - i) * tq
        for hd in range(HEADS_PER_BRANCH):
            cols = slice(hd * HEAD_DIM, (hd + 1) * HEAD_DIM)
            s = _dot_nt(q[:, cols], kb[:, cols]) - f_ref[0, hd:hd + 1, pl.ds(k0, tq)]
            s = jnp.where(valid, s, -jnp.inf)
            m_old = m_ref[hd]
            m_new = jnp.maximum(m_old, jnp.max(s, axis=-1, keepdims=True))
            alpha = jnp.exp(m_old - m_new)
            p = jnp.exp(s - m_new)
            l_ref[hd] = alpha * l_ref[hd] + jnp.sum(p, axis=-1, keepdims=True)
            acc_ref[:, cols] = alpha * acc_ref[:, cols] + _dot(p.astype(BF16), vb[:, cols])
            m_ref[hd] = m_new
        return carry

    lax.fori_loop(0, i + 1, body, 0)
    for hd in range(HEADS_PER_BRANCH):
        cols = slice(hd * HEAD_DIM, (hd + 1) * HEAD_DIM)
        o_ref[0, :, cols] = acc_ref[:, cols] / l_ref[hd]


def _fox_attention(q, k, v, fcum, tq):
    b, s, w = q.shape
    return pl.pallas_call(
        functools.partial(_fox_kernel, tq=tq),
        grid=(b, s // tq),
        in_specs=[
            pl.BlockSpec((1, tq, w), lambda i, j: (i, j, 0)),
            pl.BlockSpec((1, s, w), lambda i, j: (i, 0, 0)),
            pl.BlockSpec((1, s, w), lambda i, j: (i, 0, 0)),
            pl.BlockSpec((1, FORGET_ROWS, s), lambda i, j: (i, 0, 0)),
        ],
        out_specs=pl.BlockSpec((1, tq, w), lambda i, j: (i, j, 0)),
        out_shape=jax.ShapeDtypeStruct((b, s, w), F32),
        scratch_shapes=[pltpu.VMEM((tq, w), F32), pltpu.VMEM((HEADS_PER_BRANCH, tq, 1), F32),
                        pltpu.VMEM((HEADS_PER_BRANCH, tq, 1), F32)],
        compiler_params=pltpu.CompilerParams(dimension_semantics=("arbitrary", "arbitrary"),
                                             vmem_limit_bytes=V7X_VMEM_LIMIT_BYTES),
        name="fox_attention",
    )(q, k, v, fcum)


def _dil_kernel(q_ref, kp_ref, ko_ref, vp_ref, vo_ref, o_ref, lse_ref, *, n_back):
    ub = pl.program_id(1)
    q = q_ref[0]
    kcat = jnp.concatenate([kp_ref[0], ko_ref[0]], axis=0)
    vcat = jnp.concatenate([vp_ref[0], vo_ref[0]], axis=0)
    qi = lax.broadcasted_iota(jnp.int32, (DIL_BLOCK, 2 * DIL_BLOCK), 0)
    kc = lax.broadcasted_iota(jnp.int32, (DIL_BLOCK, 2 * DIL_BLOCK), 1)
    dist = qi + DIL_BLOCK - kc
    valid = (dist >= 0) & (dist <= n_back) & ((ub - 1) * DIL_BLOCK + kc >= 0)
    for hd in range(HEADS_PER_BRANCH):
        cols = slice(hd * HEAD_DIM, (hd + 1) * HEAD_DIM)
        s = jnp.where(valid, _dot_nt(q[:, cols], kcat[:, cols]), -jnp.inf)
        m = jnp.max(s, axis=-1, keepdims=True)
        p = jnp.exp(s - m)
        denom = jnp.sum(p, axis=-1, keepdims=True)
        o_ref[0, :, cols] = _dot(p.astype(BF16), vcat[:, cols]) / denom
        lse_ref[0, :, cols] = jnp.broadcast_to(m + jnp.log(denom), (DIL_BLOCK, HEAD_DIM))


def _dil_attention(q, k, v, group, window, dilation):
    b, s, w = q.shape
    l = s // dilation
    n_groups = w // BRANCH_WIDTH
    view = lambda t: t.reshape(b, l, dilation * w)

    def cur(i, u, r):
        return (i, u, r * n_groups + group)

    def prev(i, u, r):
        return (i, jnp.maximum(u - 1, 0), r * n_groups + group)

    blk = (1, DIL_BLOCK, BRANCH_WIDTH)
    out = jax.ShapeDtypeStruct((b, l, dilation * BRANCH_WIDTH), F32)
    o, lse = pl.pallas_call(
        functools.partial(_dil_kernel, n_back=window // dilation),
        grid=(b, l // DIL_BLOCK, dilation),
        in_specs=[pl.BlockSpec(blk, cur), pl.BlockSpec(blk, prev), pl.BlockSpec(blk, cur),
                  pl.BlockSpec(blk, prev), pl.BlockSpec(blk, cur)],
        out_specs=[pl.BlockSpec(blk, lambda i, u, r: (i, u, r))] * 2,
        out_shape=[out, out],
        compiler_params=pltpu.CompilerParams(dimension_semantics=("arbitrary",) * 3),
        name=f"dil_attention_d{dilation}",
    )(view(q), view(k), view(k), view(v), view(v))
    return o.reshape(b, s, BRANCH_WIDTH), lse.reshape(b, s, BRANCH_WIDTH)


def _out_kernel(x_ref, g_ref, o_sb_ref, o_fx_ref, o0_ref, l0_ref, o1_ref, l1_ref, o2_ref, l2_ref,
                q_mm_ref, km_ref, vm_ref, z_ref, wg_ref, wbr_ref, wout_ref, y_ref):
    x = x_ref[0]
    h = _rms_rows(x, g_ref[...]).astype(BF16)

    l0, l1, l2 = l0_ref[0], l1_ref[0], l2_ref[0]
    m = jnp.maximum(jnp.maximum(l0, l1), l2)
    e0, e1, e2 = jnp.exp(l0 - m), jnp.exp(l1 - m), jnp.exp(l2 - m)
    o_dl = (e0 * o0_ref[0] + e1 * o1_ref[0] + e2 * o2_ref[0]) / (e0 + e1 + e2)

    q = q_mm_ref[0]
    km = km_ref[0]
    vm = vm_ref[0]
    heads = []
    for hd in range(HEADS_PER_BRANCH):
        cols = slice(hd * HEAD_DIM, (hd + 1) * HEAD_DIM)
        s = _dot_nt(q[:, cols], km[:, cols])
        p = jnp.exp(s - jnp.max(s, axis=-1, keepdims=True))
        heads.append(_dot(p.astype(BF16), vm[:, cols]) / jnp.sum(p, axis=-1, keepdims=True))
    o_mm = jnp.concatenate(heads, axis=1)

    merged = None
    for br, o in enumerate((o_sb_ref[0], o_fx_ref[0], o_dl, o_mm)):
        gated = (o * z_ref[0, :, br * BRANCH_WIDTH:(br + 1) * BRANCH_WIDTH]).astype(BF16)
        y = _dot(gated, wbr_ref[br])
        gate = _sigmoid(_dot(h, wg_ref[:, br * D_MODEL:(br + 1) * D_MODEL]))
        merged = gate * y if merged is None else merged + gate * y
    y_ref[0] = x + _dot(merged.astype(BF16), wout_ref[...])


def _out_layer(x, ln_g, o_sb, o_fx, dil, q_mm, km, vm, z, w_gate, w_br, w_out, tm):
    b, s, _ = x.shape
    bw = BRANCH_WIDTH
    n_mem = km.shape[1]

    def tok_spec(width):
        return pl.BlockSpec((1, tm, width), lambda i, j: (i, j, 0))

    def const_spec(shape):
        return pl.BlockSpec(shape, lambda i, j: (0,) * len(shape))

    mem_spec = pl.BlockSpec((1, n_mem, bw), lambda i, j: (i, 0, 0))
    dil_args = [t for pair in dil for t in pair]
    return pl.pallas_call(
        _out_kernel,
        grid=(b, s // tm),
        in_specs=([tok_spec(D_MODEL), const_spec((1, D_MODEL)), tok_spec(bw), tok_spec(bw)]
                  + [tok_spec(bw)] * (2 * N_DIL)
                  + [tok_spec(bw), mem_spec, mem_spec, tok_spec(N_BRANCH * bw),
                     const_spec((D_MODEL, N_BRANCH * D_MODEL)), const_spec((N_BRANCH, bw, D_MODEL)),
                     const_spec((D_MODEL, D_MODEL))]),
        out_specs=tok_spec(D_MODEL),
        out_shape=jax.ShapeDtypeStruct((b, s, D_MODEL), F32),
        compiler_params=pltpu.CompilerParams(dimension_semantics=("arbitrary", "arbitrary"),
                                             vmem_limit_bytes=V7X_VMEM_LIMIT_BYTES),
        name="out_layer",
    )(x, ln_g, o_sb, o_fx, *dil_args, q_mm, km, vm, z, w_gate, w_br, w_out)


def _tile(n, pref):
    t = min(pref, n)
    assert n % t == 0, (n, t)
    return t


def _rope_tables(s):
    half = HEAD_DIM // 2
    inv = ROPE_THETA ** (-jnp.arange(half, dtype=F32) / half)
    ang = jnp.arange(s, dtype=F32)[:, None] * inv[None, :]
    cos, sin = jnp.cos(ang), jnp.sin(ang)
    cos_t = jnp.tile(jnp.concatenate([cos, cos], axis=1), (1, HEADS_PER_BRANCH))
    sin_t = jnp.tile(jnp.concatenate([-sin, sin], axis=1), (1, HEADS_PER_BRANCH))
    return cos_t, sin_t


def _split_w_in(w):
    bw = BRANCH_WIDTH
    widths = (bw,) * 4 + (bw,) * 4 + (HEADS_PER_BRANCH,) + (N_DIL * bw,) * 3 + (bw, bw, bw, N_BRANCH * D_MODEL)
    assert sum(widths) == w.shape[1], (sum(widths), w.shape)
    offs = np.cumsum((0,) + widths)
    return [w[:, offs[i]:offs[i + 1]] for i in range(len(widths))]


def kernel(x, mem, ln_gain, mem_ln_gain, qk_gain, w_in, b_forget, w_mem_kv, w_br_sb, w_br_fox, w_br_dil,
           w_br_mem, w_out):
    depth = w_in.shape[0]
    b, s, _ = x.shape
    bw = BRANCH_WIDTH
    tm_proj = _tile(s, 512)
    tq = _tile(s, 256)
    tm_out = _tile(s, 256)

    head_id = np.arange(bw) // HEAD_DIM
    group_mean = jnp.asarray((head_id[:, None] == head_id[None, :]) / HEAD_DIM, BF16)
    tri_prefix = jnp.asarray(np.arange(tm_proj)[:, None] <= np.arange(tm_proj)[None, :], BF16)
    tri_later = jnp.asarray(np.arange(tq)[:, None] > np.arange(tq)[None, :], BF16)
    cos_t, sin_t = _rope_tables(s)

    gains = jnp.tile(qk_gain, (1, 1, HEADS_PER_BRANCH))
    km_all, vm_all = _mem_kv(mem, mem_ln_gain[:, None, :], w_mem_kv.astype(BF16), gains[:, 5:6, :], group_mean)

    for l in range(depth):
        (q_sb, k_sb, v_sb, z_sb, q_fx, k_fx, v_fx, z_fx, f_fx, q_dl, k_dl, v_dl, z_dl, q_mm, z_mm,
         w_gate) = _split_w_in(w_in[l])
        w_main = jnp.concatenate([q_sb, k_sb, v_sb, q_fx, k_fx, v_fx, q_dl, k_dl, v_dl, q_mm,
                                  z_sb, z_fx, z_dl, z_mm], axis=1).astype(BF16)
        wf_t = jnp.pad(f_fx.T, ((0, FORGET_ROWS - HEADS_PER_BRANCH), (0, 0))).astype(BF16)
        bf = jnp.pad(b_forget[l], (0, FORGET_ROWS - HEADS_PER_BRANCH))[:, None]
        gains_l = jnp.pad(gains[l], ((0, 2), (0, 0)))
        ln_g = ln_gain[l][None, :]

        (q_sb, k_sb, v_sb, q_fx, k_fx, v_fx, q_dl, k_dl, v_dl, q_mm, z, fcum) = _proj(
            x, ln_g, w_main, wf_t, bf, gains_l, cos_t, sin_t, group_mean, tri_prefix, tm_proj)

        o_sb = _sb_attention(q_sb, k_sb, v_sb, tri_later, tq)
        o_fx = _fox_attention(q_fx, k_fx, v_fx, fcum, tq)
        dil = [_dil_attention(q_dl, k_dl, v_dl, g, window, dilation)
               for g, (window, dilation) in enumerate(DIL_PATTERNS)]
        w_br = jnp.stack([w_br_sb[l], w_br_fox[l], w_br_dil[l], w_br_mem[l]]).astype(BF16)
        x = _out_layer(x, ln_g, o_sb, o_fx, dil, q_mm, km_all[l], vm_all[l], z, w_gate.astype(BF16), w_br,
                       w_out[l].astype(BF16), tm_out)
    return x
```

```python
import functools

import jax
import jax.numpy as jnp
import numpy as np
from jax import lax
from jax.experimental import pallas as pl
from jax.experimental.pallas import tpu as pltpu

F32 = jnp.float32
BF16 = jnp.bfloat16

D_MODEL = 1024
HEAD_DIM = 64
HEADS_PER_BRANCH = 4
BRANCH_WIDTH = HEADS_PER_BRANCH * HEAD_DIM
DIL_PATTERNS = ((128, 1), (512, 4), (2048, 16))
N_DIL = len(DIL_PATTERNS)
N_BRANCH = 4
DIL_BLOCK = 128
DIL_SPAN = 2048
ROPE_THETA = 10000.0
EPS = 1e-6
MASKED_LOGIT = -1e4
QK_SCALE = HEAD_DIM ** -0.5
SB_UNDERFLOW = -105.0
V7X_VMEM_LIMIT_BYTES = 56 * 1024 * 1024
V7X_LANES = 128
V7X_BF16_SUBLANES = 16
FOX_HEAD_LANES = V7X_LANES
FOX_BIAS_TERMS = 3
FOX_VT_ROWS = HEAD_DIM + V7X_BF16_SUBLANES
FOX_BLOCK = 256

_NT = (((1,), (1,)), ((), ()))


def _dot(a, b):
    return jnp.dot(a, b, preferred_element_type=F32)


def _dot_nt(a, b):
    return lax.dot_general(a, b, _NT, preferred_element_type=F32)


def _bf16_terms(x, terms):
    out = []
    r = x
    for t in range(terms):
        p = r.astype(BF16)
        out.append(p)
        if t + 1 < terms:
            r = r - p.astype(F32)
    return out


def _rms_rows(x, gain):
    ms = jnp.mean(x * x, axis=-1, keepdims=True)
    return x * lax.rsqrt(ms + EPS) * gain


def _head_rms(x, group_mean, gain):
    ms = sum(_dot(p, group_mean) for p in _bf16_terms(x * x, 2))
    return x * lax.rsqrt(ms + EPS) * gain


def _sigmoid(x):
    return 1.0 / (1.0 + jnp.exp(-x))


def _log_sigmoid(x):
    return jnp.minimum(x, 0.0) - jnp.log(1.0 + jnp.exp(-jnp.abs(x)))


def _rope(x, cos, sin_signed):
    lane = lax.broadcasted_iota(jnp.int32, x.shape, 1)
    first_half = (lane & (HEAD_DIM - 1)) < (HEAD_DIM // 2)
    w = x.shape[1]
    partner = jnp.where(first_half, pltpu.roll(x, w - HEAD_DIM // 2, 1), pltpu.roll(x, HEAD_DIM // 2, 1))
    return x * cos + partner * sin_signed


def _mem_kv_kernel(mem_ref, g_ref, w_ref, gk_ref, gm_ref, km_ref, vm_ref):
    h = _rms_rows(mem_ref[0], g_ref[0]).astype(BF16)
    kv = _dot(h, w_ref[0])
    km_ref[0, 0] = _head_rms(kv[:, :BRANCH_WIDTH], gm_ref[...], gk_ref[0]).astype(BF16)
    vm_ref[0, 0] = kv[:, BRANCH_WIDTH:].astype(BF16)


def _mem_kv(mem, mem_ln_gain, w_mem_kv, gain_k, group_mean):
    depth = w_mem_kv.shape[0]
    b, n_mem, _ = mem.shape
    out = jax.ShapeDtypeStruct((depth, b, n_mem, BRANCH_WIDTH), BF16)
    return pl.pallas_call(
        _mem_kv_kernel,
        grid=(depth, b),
        in_specs=[
            pl.BlockSpec((1, n_mem, D_MODEL), lambda l, i: (i, 0, 0)),
            pl.BlockSpec((1, 1, D_MODEL), lambda l, i: (l, 0, 0)),
            pl.BlockSpec((1, D_MODEL, 2 * BRANCH_WIDTH), lambda l, i: (l, 0, 0)),
            pl.BlockSpec((1, 1, BRANCH_WIDTH), lambda l, i: (l, 0, 0)),
            pl.BlockSpec((BRANCH_WIDTH, BRANCH_WIDTH), lambda l, i: (0, 0)),
        ],
        out_specs=[pl.BlockSpec((1, 1, n_mem, BRANCH_WIDTH), lambda l, i: (l, i, 0, 0))] * 2,
        out_shape=[out, out],
        compiler_params=pltpu.CompilerParams(dimension_semantics=("arbitrary", "arbitrary")),
        name="mem_kv",
    )(mem, mem_ln_gain, w_mem_kv, gain_k, group_mean)


_SEC_Q_SB, _SEC_K_SB, _SEC_V_SB = 0, 1, 2
_SEC_Q_FX, _SEC_K_FX, _SEC_V_FX = 3, 4, 5
_SEC_Q_DL, _SEC_K_DL, _SEC_V_DL = 6, 9, 12
_SEC_Q_MM = 15
_SEC_Z = 16
_N_SEC = 20


def _proj_kernel(x_ref, g_ref, w_ref, wf_ref, bf_ref, gains_ref, cos_ref, sin_ref, gm_ref, tri_ref,
                 place_ref, place_f_ref, q_bias_ref,
                 q_sb_ref, k_sb_ref, v_sb_ref, q_fx_ref, k_fx_ref, vt_fx_ref, *rest):
    dil_refs = rest[:3 * N_DIL]
    q_mm_ref, z_ref, stride_ref, carry_ref = rest[3 * N_DIL:]
    tm = x_ref.shape[1]
    h = _rms_rows(x_ref[0], g_ref[...]).astype(BF16)
    gm = gm_ref[...]

    def sec(i):
        return _dot(h, w_ref[:, i * BRANCH_WIDTH:(i + 1) * BRANCH_WIDTH])

    def gain(i):
        return gains_ref[i:i + 1, :]

    q_sb_ref[0] = (sec(_SEC_Q_SB) * QK_SCALE).astype(BF16)
    k_sb_ref[0] = sec(_SEC_K_SB).astype(BF16)
    v_sb_ref[0] = sec(_SEC_V_SB).astype(BF16)

    @pl.when(pl.program_id(1) == 0)
    def _():
        carry_ref[...] = jnp.zeros_like(carry_ref)

    log_f = _log_sigmoid(_dot(h, wf_ref[...]) + bf_ref[...])
    tri = tri_ref[...]
    fcum = sum(_dot(tri, p) for p in _bf16_terms(log_f, 3)) + carry_ref[...]
    carry_ref[...] = fcum[tm - 1:tm, :]
    place = place_ref[...]
    q_fx = (_head_rms(sec(_SEC_Q_FX), gm, gain(0)) * QK_SCALE).astype(BF16)
    q_fx_ref[0] = (_dot(q_fx, place) + q_bias_ref[...]).astype(BF16)
    k_fx = _dot(_head_rms(sec(_SEC_K_FX), gm, gain(1)).astype(BF16), place)
    for t, p in enumerate(_bf16_terms(fcum, FOX_BIAS_TERMS)):
        k_fx = k_fx + _dot(p, place_f_ref[t])
    k_fx_ref[0] = k_fx.astype(BF16)
    vt = sec(_SEC_V_FX).T.astype(BF16)
    ones_rows = (lax.broadcasted_iota(jnp.int32, (V7X_BF16_SUBLANES, FOX_BLOCK), 0) == 0).astype(BF16)
    for blk in range(tm // FOX_BLOCK):
        for hd in range(HEADS_PER_BRANCH):
            r0 = hd * FOX_VT_ROWS
            vt_fx_ref[0, blk, r0:r0 + HEAD_DIM, :] = vt[hd * HEAD_DIM:(hd + 1) * HEAD_DIM,
                                                        blk * FOX_BLOCK:(blk + 1) * FOX_BLOCK]
            vt_fx_ref[0, blk, r0 + HEAD_DIM:r0 + FOX_VT_ROWS, :] = ones_rows

    cos = cos_ref[...]
    sin = sin_ref[...]
    for c, (_, dilation) in enumerate(DIL_PATTERNS):
        q = _rope(_head_rms(sec(_SEC_Q_DL + c), gm, gain(2)), cos, sin) * QK_SCALE
        k = _rope(_head_rms(sec(_SEC_K_DL + c), gm, gain(3)), cos, sin)
        v = sec(_SEC_V_DL + c)
        for t, val in enumerate((q, k, v)):
            out_ref = dil_refs[3 * c + t]
            if dilation == 1:
                out_ref[0, 0] = val.astype(BF16)
            else:
                for half in range(BRANCH_WIDTH // V7X_LANES):
                    stride_ref[half] = val[:, half * V7X_LANES:(half + 1) * V7X_LANES]
                for r in range(dilation):
                    rows = pl.ds(r, tm // dilation, stride=dilation)
                    out_ref[0, r] = jnp.concatenate(
                        [stride_ref[half, rows, :] for half in range(BRANCH_WIDTH // V7X_LANES)], axis=1).astype(BF16)

    q_mm_ref[0] = (_head_rms(sec(_SEC_Q_MM), gm, gain(4)) * QK_SCALE).astype(BF16)

    for c in range(N_BRANCH):
        z = sec(_SEC_Z + c)
        z_ref[0, :, c * BRANCH_WIDTH:(c + 1) * BRANCH_WIDTH] = z * _sigmoid(z)


def _proj(x, ln_g, w_main, wf, bf, gains, cos_t, sin_t, group_mean, tri, place, place_f, q_bias, tm):
    b, s, _ = x.shape
    bw = BRANCH_WIDTH
    fox_w = HEADS_PER_BRANCH * FOX_HEAD_LANES
    vt_rows = HEADS_PER_BRANCH * FOX_VT_ROWS

    def tok(width, dtype):
        return jax.ShapeDtypeStruct((b, s, width), dtype)

    def tok_spec(width):
        return pl.BlockSpec((1, tm, width), lambda i, j: (i, j, 0))

    def const_spec(shape):
        return pl.BlockSpec(shape, lambda i, j: (0,) * len(shape))

    dil_shapes, dil_specs = [], []
    for _, d in DIL_PATTERNS:
        dil_shapes += [jax.ShapeDtypeStruct((b, d, s // d, bw), BF16)] * 3
        dil_specs += [pl.BlockSpec((1, d, tm // d, bw), lambda i, j: (i, 0, j, 0))] * 3

    out_shape = ([tok(bw, BF16)] * 3 + [tok(fox_w, BF16)] * 2
                 + [jax.ShapeDtypeStruct((b, s // FOX_BLOCK, vt_rows, FOX_BLOCK), BF16)]
                 + dil_shapes + [tok(bw, BF16), tok(N_BRANCH * bw, F32)])
    out_specs = ([tok_spec(bw)] * 3 + [tok_spec(fox_w)] * 2
                 + [pl.BlockSpec((1, tm // FOX_BLOCK, vt_rows, FOX_BLOCK), lambda i, j: (i, j, 0, 0))]
                 + dil_specs + [tok_spec(bw), tok_spec(N_BRANCH * bw)])
    return pl.pallas_call(
        _proj_kernel,
        grid=(b, s // tm),
        in_specs=[
            tok_spec(D_MODEL),
            const_spec((1, D_MODEL)),
            const_spec((D_MODEL, _N_SEC * bw)),
            const_spec((D_MODEL, V7X_LANES)),
            const_spec((1, V7X_LANES)),
            const_spec((8, bw)),
            pl.BlockSpec((tm, bw), lambda i, j: (j, 0)),
            pl.BlockSpec((tm, bw), lambda i, j: (j, 0)),
            const_spec((bw, bw)),
            const_spec((tm, tm)),
            const_spec((bw, fox_w)),
            const_spec((FOX_BIAS_TERMS, V7X_LANES, fox_w)),
            const_spec((1, fox_w)),
        ],
        out_specs=out_specs,
        out_shape=out_shape,
        scratch_shapes=[pltpu.VMEM((bw // V7X_LANES, tm, V7X_LANES), F32), pltpu.VMEM((1, V7X_LANES), F32)],
        compiler_params=pltpu.CompilerParams(dimension_semantics=("arbitrary", "arbitrary"),
                                             vmem_limit_bytes=V7X_VMEM_LIMIT_BYTES),
        name="proj",
    )(x, ln_g, w_main, wf, bf, gains, cos_t, sin_t, group_mean, tri, place, place_f, q_bias)


def _sb_kernel(q_ref, k_ref, v_ref, tri_ref, o_ref, acc_ref, c_ref, *, tq):
    i = pl.program_id(1)
    q = q_ref[0]
    tri = tri_ref[...]
    acc_ref[...] = jnp.zeros_like(acc_ref)
    c_ref[...] = jnp.zeros_like(c_ref)
    row = lax.broadcasted_iota(jnp.int32, (tq, tq), 0)
    col = lax.broadcasted_iota(jnp.int32, (tq, tq), 1)
    delta = row - col

    def cond(state):
        j, c_max = state
        return jnp.logical_and(j >= 0, c_max > SB_UNDERFLOW)

    def body(state):
        j, _ = state
        k0 = pl.multiple_of(j * tq, tq)
        kb = k_ref[0, pl.ds(k0, tq), :]
        vb = v_ref[0, pl.ds(k0, tq), :]
        valid = delta > (j - i) * tq
        c_max = jnp.float32(-jnp.inf)
        for hd in range(HEADS_PER_BRANCH):
            cols = slice(hd * HEAD_DIM, (hd + 1) * HEAD_DIM)
            z = jnp.where(valid, _dot_nt(q[:, cols], kb[:, cols]), MASKED_LOGIT)
            t = jnp.log(1.0 + jnp.exp(-jnp.abs(z)))
            log_beta = jnp.minimum(z, 0.0) - t
            log_keep = -jnp.maximum(z, 0.0) - t
            c = c_ref[hd]
            later = _dot(log_keep.astype(BF16), tri) + c
            a = jnp.exp(log_beta + later)
            acc_ref[:, cols] += _dot(a.astype(BF16), vb[:, cols])
            c = c + jnp.sum(log_keep, axis=-1, keepdims=True)
            c_ref[hd] = c
            c_max = jnp.maximum(c_max, jnp.max(c))
        return j - 1, c_max

    lax.while_loop(cond, body, (i, jnp.float32(0.0)))
    o_ref[0] = acc_ref[...]


def _sb_attention(q, k, v, tri_strict, tq):
    b, s, w = q.shape
    return pl.pallas_call(
        functools.partial(_sb_kernel, tq=tq),
        grid=(b, s // tq),
        in_specs=[
            pl.BlockSpec((1, tq, w), lambda i, j: (i, j, 0)),
            pl.BlockSpec((1, s, w), lambda i, j: (i, 0, 0)),
            pl.BlockSpec((1, s, w), lambda i, j: (i, 0, 0)),
            pl.BlockSpec((tq, tq), lambda i, j: (0, 0)),
        ],
        out_specs=pl.BlockSpec((1, tq, w), lambda i, j: (i, j, 0)),
        out_shape=jax.ShapeDtypeStruct((b, s, w), F32),
        scratch_shapes=[pltpu.VMEM((tq, w), F32), pltpu.VMEM((HEADS_PER_BRANCH, tq, 1), F32)],
        compiler_params=pltpu.CompilerParams(dimension_semantics=("arbitrary", "arbitrary"),
                                             vmem_limit_bytes=V7X_VMEM_LIMIT_BYTES),
        name="sb_attention",
    )(q, k, v, tri_strict)


def _fox_kernel(q_ref, k_ref, vt_ref, o_ref, acc_ref, m_ref):
    i = pl.program_id(1)
    tq = q_ref.shape[1]
    q = q_ref[0]
    acc_ref[...] = jnp.zeros_like(acc_ref)
    m_ref[...] = jnp.full_like(m_ref, -jnp.inf)

    def step(j, masked):
        k0 = pl.multiple_of(j * FOX_BLOCK, FOX_BLOCK)
        kb = k_ref[0, pl.ds(k0, FOX_BLOCK), :]
        heads = range(HEADS_PER_BRANCH)
        scores = []
        for hd in heads:
            lanes = slice(hd * FOX_HEAD_LANES, (hd + 1) * FOX_HEAD_LANES)
            s = _dot_nt(kb[:, lanes], q[:, lanes])
            if masked:
                key = lax.broadcasted_iota(jnp.int32, s.shape, 0)
                qry = lax.broadcasted_iota(jnp.int32, s.shape, 1)
                s = jnp.where(key <= qry, s, -jnp.inf)
            scores.append(s)
        m_old = [m_ref[hd] for hd in heads]
        m_new = [jnp.maximum(m_old[hd], jnp.max(scores[hd], axis=0, keepdims=True)) for hd in heads]
        probs = [jnp.exp(scores[hd] - m_new[hd]).astype(BF16) for hd in heads]
        pv = [_dot(vt_ref[0, j, hd * FOX_VT_ROWS:(hd + 1) * FOX_VT_ROWS, :], probs[hd]) for hd in heads]
        for hd in heads:
            acc_ref[hd] = jnp.exp(m_old[hd] - m_new[hd]) * acc_ref[hd] + pv[hd]
            m_ref[hd] = m_new[hd]

    def body(j, carry):
        step(j, False)
        return carry

    lax.fori_loop(0, i, body, 0)
    step(i, True)
    for hd in range(HEADS_PER_BRANCH):
        acc = acc_ref[hd]
        o_t = acc[:HEAD_DIM] / acc[HEAD_DIM:HEAD_DIM + 1]
        o_ref[0, :, hd * HEAD_DIM:(hd + 1) * HEAD_DIM] = o_t.T


def _fox_attention(q, k, vt):
    b, s, w = q.shape
    tq = FOX_BLOCK
    n_blk, vt_rows, _ = vt.shape[1:]
    return pl.pallas_call(
        _fox_kernel,
        grid=(b, s // tq),
        in_specs=[
            pl.BlockSpec((1, tq, w), lambda i, j: (i, j, 0)),
            pl.BlockSpec((1, s, w), lambda i, j: (i, 0, 0)),
            pl.BlockSpec((1, n_blk, vt_rows, FOX_BLOCK), lambda i, j: (i, 0, 0, 0)),
        ],
        out_specs=pl.BlockSpec((1, tq, BRANCH_WIDTH), lambda i, j: (i, j, 0)),
        out_shape=jax.ShapeDtypeStruct((b, s, BRANCH_WIDTH), F32),
        scratch_shapes=[pltpu.VMEM((HEADS_PER_BRANCH, FOX_VT_ROWS, tq), F32),
                        pltpu.VMEM((HEADS_PER_BRANCH, 1, tq), F32)],
        compiler_params=pltpu.CompilerParams(dimension_semantics=("arbitrary", "arbitrary"),
                                             vmem_limit_bytes=V7X_VMEM_LIMIT_BYTES),
        name="fox_attention",
    )(q, k, vt)


def _dil_kernel(q_ref, kc_ref, kp_ref, vc_ref, vp_ref, o_ref, lse_ref, stride_ref, *, n_back, dilation):
    span_u = q_ref.shape[2]
    n_blk = span_u // DIL_BLOCK
    qi = lax.broadcasted_iota(jnp.int32, (DIL_BLOCK, 2 * DIL_BLOCK), 0)
    kc = lax.broadcasted_iota(jnp.int32, (DIL_BLOCK, 2 * DIL_BLOCK), 1)
    dist = qi + DIL_BLOCK - kc
    in_window = (dist >= 0) & (dist <= n_back)
    first_key = jnp.where(pl.program_id(1) == 0, DIL_BLOCK, 0)
    in_window_first = in_window & (kc >= first_key)
    for r in range(dilation):
        for ub in range(n_blk):
            rows = slice(ub * DIL_BLOCK, (ub + 1) * DIL_BLOCK)
            q = q_ref[0, r, rows, :]
            if ub == 0:
                k_prev, v_prev = kp_ref[0, r, span_u - DIL_BLOCK:, :], vp_ref[0, r, span_u - DIL_BLOCK:, :]
                valid = in_window_first
            else:
                prev = slice((ub - 1) * DIL_BLOCK, ub * DIL_BLOCK)
                k_prev, v_prev = kc_ref[0, r, prev, :], vc_ref[0, r, prev, :]
                valid = in_window
            kcat = jnp.concatenate([k_prev, kc_ref[0, r, rows, :]], axis=0)
            vcat = jnp.concatenate([v_prev, vc_ref[0, r, rows, :]], axis=0)
            outs, lses = [], []
            for hd in range(HEADS_PER_BRANCH):
                cols = slice(hd * HEAD_DIM, (hd + 1) * HEAD_DIM)
                s = jnp.where(valid, _dot_nt(q[:, cols], kcat[:, cols]), -jnp.inf)
                m = jnp.max(s, axis=-1, keepdims=True)
                p = jnp.exp(s - m)
                denom = jnp.sum(p, axis=-1, keepdims=True)
                outs.append(_dot(p.astype(BF16), vcat[:, cols]) / denom)
                lses.append(jnp.broadcast_to(m + jnp.log(denom), (DIL_BLOCK, HEAD_DIM)))
            results = (jnp.concatenate(outs, axis=1), jnp.concatenate(lses, axis=1))
            if dilation == 1:
                o_ref[0, rows, :], lse_ref[0, rows, :] = results
            else:
                dst = pl.ds(ub * DIL_BLOCK * dilation + r, DIL_BLOCK, stride=dilation)
                for t, val in enumerate(results):
                    for half in range(BRANCH_WIDTH // V7X_LANES):
                        stride_ref[t, half, dst, :] = val[:, half * V7X_LANES:(half + 1) * V7X_LANES]
    if dilation > 1:
        for t, out_ref in enumerate((o_ref, lse_ref)):
            for half in range(BRANCH_WIDTH // V7X_LANES):
                out_ref[0, :, half * V7X_LANES:(half + 1) * V7X_LANES] = stride_ref[t, half]


def _dil_attention(q, k, v, window, dilation):
    b, _, l, w = q.shape
    s = l * dilation
    span = min(DIL_SPAN, s)
    assert s % span == 0 and (span // dilation) % DIL_BLOCK == 0, (s, span, dilation)
    blk = (1, dilation, span // dilation, w)

    def cur(i, j):
        return (i, 0, j, 0)

    def prev(i, j):
        return (i, 0, jnp.maximum(j - 1, 0), 0)

    out = jax.ShapeDtypeStruct((b, s, w), F32)
    return pl.pallas_call(
        functools.partial(_dil_kernel, n_back=window // dilation, dilation=dilation),
        grid=(b, s // span),
        in_specs=[pl.BlockSpec(blk, cur), pl.BlockSpec(blk, cur), pl.BlockSpec(blk, prev),
                  pl.BlockSpec(blk, cur), pl.BlockSpec(blk, prev)],
        out_specs=[pl.BlockSpec((1, span, w), lambda i, j: (i, j, 0))] * 2,
        out_shape=[out, out],
        scratch_shapes=[pltpu.VMEM((2, w // V7X_LANES, span, V7X_LANES), F32)],
        compiler_params=pltpu.CompilerParams(dimension_semantics=("arbitrary", "arbitrary"),
                                             vmem_limit_bytes=V7X_VMEM_LIMIT_BYTES),
        name=f"dil_attention_d{dilation}",
    )(q, k, k, v, v)


def _out_kernel(x_ref, g_ref, o_sb_ref, o_fx_ref, o0_ref, l0_ref, o1_ref, l1_ref, o2_ref, l2_ref,
                q_mm_ref, km_ref, vm_ref, z_ref, wg_ref, wbr_ref, wout_ref, y_ref):
    x = x_ref[0]
    h = _rms_rows(x, g_ref[...]).astype(BF16)

    l0, l1, l2 = l0_ref[0], l1_ref[0], l2_ref[0]
    m = jnp.maximum(jnp.maximum(l0, l1), l2)
    e0, e1, e2 = jnp.exp(l0 - m), jnp.exp(l1 - m), jnp.exp(l2 - m)
    o_dl = (e0 * o0_ref[0] + e1 * o1_ref[0] + e2 * o2_ref[0]) / (e0 + e1 + e2)

    q = q_mm_ref[0]
    km = km_ref[0]
    vm = vm_ref[0]
    heads = []
    for hd in range(HEADS_PER_BRANCH):
        cols = slice(hd * HEAD_DIM, (hd + 1) * HEAD_DIM)
        s = _dot_nt(q[:, cols], km[:, cols])
        p = jnp.exp(s - jnp.max(s, axis=-1, keepdims=True))
        heads.append(_dot(p.astype(BF16), vm[:, cols]) / jnp.sum(p, axis=-1, keepdims=True))
    o_mm = jnp.concatenate(heads, axis=1)

    merged = None
    for br, o in enumerate((o_sb_ref[0], o_fx_ref[0], o_dl, o_mm)):
        gated = (o * z_ref[0, :, br * BRANCH_WIDTH:(br + 1) * BRANCH_WIDTH]).astype(BF16)
        y = _dot(gated, wbr_ref[br])
        gate = _sigmoid(_dot(h, wg_ref[:, br * D_MODEL:(br + 1) * D_MODEL]))
        merged = gate * y if merged is None else merged + gate * y
    y_ref[0] = x + _dot(merged.astype(BF16), wout_ref[...])


def _out_layer(x, ln_g, o_sb, o_fx, dil, q_mm, km, vm, z, w_gate, w_br, w_out, tm):
    b, s, _ = x.shape
    bw = BRANCH_WIDTH
    n_mem = km.shape[1]

    def tok_spec(width):
        return pl.BlockSpec((1, tm, width), lambda i, j: (i, j, 0))

    def const_spec(shape):
        return pl.BlockSpec(shape, lambda i, j: (0,) * len(shape))

    mem_spec = pl.BlockSpec((1, n_mem, bw), lambda i, j: (i, 0, 0))
    dil_args = [t for pair in dil for t in pair]
    return pl.pallas_call(
        _out_kernel,
        grid=(b, s // tm),
        in_specs=([tok_spec(D_MODEL), const_spec((1, D_MODEL)), tok_spec(bw), tok_spec(bw)]
                  + [tok_spec(bw)] * (2 * N_DIL)
                  + [tok_spec(bw), mem_spec, mem_spec, tok_spec(N_BRANCH * bw),
                     const_spec((D_MODEL, N_BRANCH * D_MODEL)), const_spec((N_BRANCH, bw, D_MODEL)),
                     const_spec((D_MODEL, D_MODEL))]),
        out_specs=tok_spec(D_MODEL),
        out_shape=jax.ShapeDtypeStruct((b, s, D_MODEL), F32),
        compiler_params=pltpu.CompilerParams(dimension_semantics=("arbitrary", "arbitrary"),
                                             vmem_limit_bytes=V7X_VMEM_LIMIT_BYTES),
        name="out_layer",
    )(x, ln_g, o_sb, o_fx, *dil_args, q_mm, km, vm, z, w_gate, w_br, w_out)


def _tile(n, pref):
    t = min(pref, n)
    assert n % t == 0, (n, t)
    return t


def _rope_tables(s):
    half = HEAD_DIM // 2
    inv = ROPE_THETA ** (-jnp.arange(half, dtype=F32) / half)
    ang = jnp.arange(s, dtype=F32)[:, None] * inv[None, :]
    cos, sin = jnp.cos(ang), jnp.sin(ang)
    cos_t = jnp.tile(jnp.concatenate([cos, cos], axis=1), (1, HEADS_PER_BRANCH))
    sin_t = jnp.tile(jnp.concatenate([-sin, sin], axis=1), (1, HEADS_PER_BRANCH))
    return cos_t, sin_t


def _fox_placement():
    fox_w = HEADS_PER_BRANCH * FOX_HEAD_LANES
    place = np.zeros((BRANCH_WIDTH, fox_w), np.float32)
    place_f = np.zeros((FOX_BIAS_TERMS, V7X_LANES, fox_w), np.float32)
    q_bias = np.zeros((1, fox_w), np.float32)
    for hd in range(HEADS_PER_BRANCH):
        for d in range(HEAD_DIM):
            place[hd * HEAD_DIM + d, hd * FOX_HEAD_LANES + d] = 1.0
        for t in range(FOX_BIAS_TERMS):
            place_f[t, hd, hd * FOX_HEAD_LANES + HEAD_DIM + t] = 1.0
            q_bias[0, hd * FOX_HEAD_LANES + HEAD_DIM + t] = -1.0
    return jnp.asarray(place, BF16), jnp.asarray(place_f, BF16), jnp.asarray(q_bias, F32)


def _split_w_in(w):
    bw = BRANCH_WIDTH
    widths = (bw,) * 4 + (bw,) * 4 + (HEADS_PER_BRANCH,) + (N_DIL * bw,) * 3 + (bw, bw, bw, N_BRANCH * D_MODEL)
    assert sum(widths) == w.shape[1], (sum(widths), w.shape)
    offs = np.cumsum((0,) + widths)
    return [w[:, offs[i]:offs[i + 1]] for i in range(len(widths))]


def kernel(x, mem, ln_gain, mem_ln_gain, qk_gain, w_in, b_forget, w_mem_kv, w_br_sb, w_br_fox, w_br_dil,
           w_br_mem, w_out):
    depth = w_in.shape[0]
    b, s, _ = x.shape
    bw = BRANCH_WIDTH
    tm_proj = _tile(s, 512)
    tq = _tile(s, 256)
    tm_out = _tile(s, 256)
    assert tm_proj % FOX_BLOCK == 0 and s % FOX_BLOCK == 0

    head_id = np.arange(bw) // HEAD_DIM
    group_mean = jnp.asarray((head_id[:, None] == head_id[None, :]) / HEAD_DIM, BF16)
    tri_prefix = jnp.asarray(np.arange(tm_proj)[:, None] >= np.arange(tm_proj)[None, :], BF16)
    tri_later = jnp.asarray(np.arange(tq)[:, None] > np.arange(tq)[None, :], BF16)
    cos_t, sin_t = _rope_tables(s)
    place, place_f, q_bias = _fox_placement()

    gains = jnp.tile(qk_gain, (1, 1, HEADS_PER_BRANCH))
    km_all, vm_all = _mem_kv(mem, mem_ln_gain[:, None, :], w_mem_kv.astype(BF16), gains[:, 5:6, :], group_mean)

    for l in range(depth):
        (q_sb, k_sb, v_sb, z_sb, q_fx, k_fx, v_fx, z_fx, f_fx, q_dl, k_dl, v_dl, z_dl, q_mm, z_mm,
         w_gate) = _split_w_in(w_in[l])
        w_main = jnp.concatenate([q_sb, k_sb, v_sb, q_fx, k_fx, v_fx, q_dl, k_dl, v_dl, q_mm,
                                  z_sb, z_fx, z_dl, z_mm], axis=1).astype(BF16)
        wf = jnp.pad(f_fx, ((0, 0), (0, V7X_LANES - HEADS_PER_BRANCH))).astype(BF16)
        bf = jnp.pad(b_forget[l], (0, V7X_LANES - HEADS_PER_BRANCH))[None, :]
        gains_l = jnp.pad(gains[l], ((0, 2), (0, 0)))
        ln_g = ln_gain[l][None, :]

        outs = _proj(x, ln_g, w_main, wf, bf, gains_l, cos_t, sin_t, group_mean, tri_prefix, place, place_f,
                     q_bias, tm_proj)
        q_sb, k_sb, v_sb, q_fx, k_fx, vt_fx = outs[:6]
        dil_qkv = outs[6:6 + 3 * N_DIL]
        q_mm, z = outs[6 + 3 * N_DIL:]

        o_sb = _sb_attention(q_sb, k_sb, v_sb, tri_later, tq)
        o_fx = _fox_attention(q_fx, k_fx, vt_fx)
        dil = [_dil_attention(*dil_qkv[3 * g:3 * g + 3], window, dilation)
               for g, (window, dilation) in enumerate(DIL_PATTERNS)]
        w_br = jnp.stack([w_br_sb[l], w_br_fox[l], w_br_dil[l], w_br_mem[l]]).astype(BF16)
        x = _out_layer(x, ln_g, o_sb, o_fx, dil, q_mm, km_all[l], vm_all[l], z, w_gate.astype(BF16), w_br,
                       w_out[l].astype(BF16), tm_out)
    return x
```

```python
import functools

import jax
import jax.numpy as jnp
import numpy as np
from jax import lax
from jax.experimental import pallas as pl
from jax.experimental.pallas import tpu as pltpu

F32 = jnp.float32
BF16 = jnp.bfloat16

D_MODEL = 1024
HEAD_DIM = 64
HEADS_PER_BRANCH = 4
BRANCH_WIDTH = HEADS_PER_BRANCH * HEAD_DIM
DIL_PATTERNS = ((128, 1), (512, 4), (2048, 16))
N_DIL = len(DIL_PATTERNS)
N_BRANCH = 4
DIL_BLOCK = 128
DIL_SPAN = 2048
ROPE_THETA = 10000.0
EPS = 1e-6
MASKED_LOGIT = -1e4
QK_SCALE = HEAD_DIM ** -0.5
SB_UNDERFLOW = -105.0
V7X_VMEM_LIMIT_BYTES = 56 * 1024 * 1024
V7X_LANES = 128
V7X_BF16_SUBLANES = 16
FOX_HEAD_LANES = V7X_LANES
FOX_BIAS_TERMS = 3
FOX_VT_ROWS = HEAD_DIM + V7X_BF16_SUBLANES
FOX_BLOCK = 512

_NT = (((1,), (1,)), ((), ()))


def _dot(a, b):
    return jnp.dot(a, b, preferred_element_type=F32)


def _dot_nt(a, b):
    return lax.dot_general(a, b, _NT, preferred_element_type=F32)


def _bf16_terms(x, terms):
    out = []
    r = x
    for t in range(terms):
        p = r.astype(BF16)
        out.append(p)
        if t + 1 < terms:
            r = r - p.astype(F32)
    return out


def _rms_rows(x, gain):
    ms = jnp.mean(x * x, axis=-1, keepdims=True)
    return x * lax.rsqrt(ms + EPS) * gain


def _head_rms(x, group_mean, gain):
    ms = sum(_dot(p, group_mean) for p in _bf16_terms(x * x, 2))
    return x * lax.rsqrt(ms + EPS) * gain


def _sigmoid(x):
    return 1.0 / (1.0 + jnp.exp(-x))


def _log_sigmoid(x):
    return jnp.minimum(x, 0.0) - jnp.log(1.0 + jnp.exp(-jnp.abs(x)))


def _rope(x, cos, sin_signed):
    lane = lax.broadcasted_iota(jnp.int32, x.shape, 1)
    first_half = (lane & (HEAD_DIM - 1)) < (HEAD_DIM // 2)
    w = x.shape[1]
    partner = jnp.where(first_half, pltpu.roll(x, w - HEAD_DIM // 2, 1), pltpu.roll(x, HEAD_DIM // 2, 1))
    return x * cos + partner * sin_signed


def _mem_kv_kernel(mem_ref, g_ref, w_ref, gk_ref, gm_ref, km_ref, vm_ref):
    h = _rms_rows(mem_ref[0], g_ref[0]).astype(BF16)
    kv = _dot(h, w_ref[0])
    km_ref[0, 0] = _head_rms(kv[:, :BRANCH_WIDTH], gm_ref[...], gk_ref[0]).astype(BF16)
    vm_ref[0, 0] = kv[:, BRANCH_WIDTH:].astype(BF16)


def _mem_kv(mem, mem_ln_gain, w_mem_kv, gain_k, group_mean):
    depth = w_mem_kv.shape[0]
    b, n_mem, _ = mem.shape
    out = jax.ShapeDtypeStruct((depth, b, n_mem, BRANCH_WIDTH), BF16)
    return pl.pallas_call(
        _mem_kv_kernel,
        grid=(depth, b),
        in_specs=[
            pl.BlockSpec((1, n_mem, D_MODEL), lambda l, i: (i, 0, 0)),
            pl.BlockSpec((1, 1, D_MODEL), lambda l, i: (l, 0, 0)),
            pl.BlockSpec((1, D_MODEL, 2 * BRANCH_WIDTH), lambda l, i: (l, 0, 0)),
            pl.BlockSpec((1, 1, BRANCH_WIDTH), lambda l, i: (l, 0, 0)),
            pl.BlockSpec((BRANCH_WIDTH, BRANCH_WIDTH), lambda l, i: (0, 0)),
        ],
        out_specs=[pl.BlockSpec((1, 1, n_mem, BRANCH_WIDTH), lambda l, i: (l, i, 0, 0))] * 2,
        out_shape=[out, out],
        compiler_params=pltpu.CompilerParams(dimension_semantics=("arbitrary", "arbitrary")),
        name="mem_kv",
    )(mem, mem_ln_gain, w_mem_kv, gain_k, group_mean)


_SEC_Q_SB, _SEC_K_SB, _SEC_V_SB, _SEC_Z_SB = 0, 1, 2, 3
_SEC_Q_FX, _SEC_K_FX, _SEC_V_FX, _SEC_Z_FX = 4, 5, 6, 7
_SEC_Q_DL, _SEC_K_DL, _SEC_V_DL, _SEC_Z_DL = 8, 11, 14, 17
_SEC_Q_MM, _SEC_Z_MM = 18, 19
_SEC_Z = (_SEC_Z_SB, _SEC_Z_FX, _SEC_Z_DL, _SEC_Z_MM)
_N_SEC = 20
_FORGET_COL = 8 * BRANCH_WIDTH
PREP_CHUNK = 512


def _prep_kernel(a_ref, b_ref, o_ref, *, first_chunk):
    shifted = jnp.concatenate([a_ref[0], b_ref[0]], axis=1)[:, HEADS_PER_BRANCH:HEADS_PER_BRANCH + PREP_CHUNK]
    if first_chunk * PREP_CHUNK >= _FORGET_COL:
        o_ref[0] = shifted.astype(BF16)
    else:
        before_forget = (pl.program_id(1) + first_chunk) * PREP_CHUNK < _FORGET_COL
        o_ref[0] = jnp.where(before_forget, a_ref[0], shifted).astype(BF16)


def _prep_weights(w_in, first_chunk, n_chunks):
    depth, d_model, _ = w_in.shape
    tail = PREP_CHUNK // V7X_LANES
    return pl.pallas_call(
        functools.partial(_prep_kernel, first_chunk=first_chunk),
        grid=(depth, n_chunks),
        in_specs=[pl.BlockSpec((1, d_model, PREP_CHUNK), lambda l, c: (l, 0, c + first_chunk)),
                  pl.BlockSpec((1, d_model, V7X_LANES), lambda l, c: (l, 0, tail * (c + first_chunk + 1)))],
        out_specs=pl.BlockSpec((1, d_model, PREP_CHUNK), lambda l, c: (l, 0, c)),
        out_shape=jax.ShapeDtypeStruct((depth, d_model, n_chunks * PREP_CHUNK), BF16),
        compiler_params=pltpu.CompilerParams(dimension_semantics=("arbitrary", "arbitrary")),
        name="prep_weights",
    )(w_in, w_in)


def _proj_kernel(x_ref, g_ref, w_ref, wf_ref, bf_ref, gains_ref, cos_ref, sin_ref, gm_ref, tri_ref,
                 place_ref, place_f_ref, q_bias_ref,
                 q_sb_ref, k_sb_ref, v_sb_ref, q_fx_ref, k_fx_ref, vt_fx_ref, *rest):
    dil_refs = rest[:3 * N_DIL]
    q_mm_ref, z_ref, stride_ref, carry_ref = rest[3 * N_DIL:]
    tm = x_ref.shape[1]
    h = _rms_rows(x_ref[0], g_ref[...]).astype(BF16)
    gm = gm_ref[...]

    def sec(i):
        return _dot(h, w_ref[:, i * BRANCH_WIDTH:(i + 1) * BRANCH_WIDTH])

    def gain(i):
        return gains_ref[i:i + 1, :]

    q_sb_ref[0] = (sec(_SEC_Q_SB) * QK_SCALE).astype(BF16)
    k_sb_ref[0] = sec(_SEC_K_SB).astype(BF16)
    v_sb_ref[0] = sec(_SEC_V_SB).astype(BF16)

    @pl.when(pl.program_id(1) == 0)
    def _():
        carry_ref[...] = jnp.zeros_like(carry_ref)

    log_f = _log_sigmoid(_dot(h, wf_ref[...]) + bf_ref[...])
    tri = tri_ref[...]
    fcum = sum(_dot(tri, p) for p in _bf16_terms(log_f, 3)) + carry_ref[...]
    carry_ref[...] = fcum[tm - 1:tm, :]
    place = place_ref[...]
    q_fx = (_head_rms(sec(_SEC_Q_FX), gm, gain(0)) * QK_SCALE).astype(BF16)
    q_fx_ref[0] = (_dot(q_fx, place) + q_bias_ref[...]).astype(BF16)
    k_fx = _dot(_head_rms(sec(_SEC_K_FX), gm, gain(1)).astype(BF16), place)
    for t, p in enumerate(_bf16_terms(fcum, FOX_BIAS_TERMS)):
        k_fx = k_fx + _dot(p, place_f_ref[t])
    k_fx_ref[0] = k_fx.astype(BF16)
    vt = sec(_SEC_V_FX).T.astype(BF16)
    ones_rows = (lax.broadcasted_iota(jnp.int32, (V7X_BF16_SUBLANES, FOX_BLOCK), 0) == 0).astype(BF16)
    for blk in range(tm // FOX_BLOCK):
        for hd in range(HEADS_PER_BRANCH):
            r0 = hd * FOX_VT_ROWS
            vt_fx_ref[0, blk, r0:r0 + HEAD_DIM, :] = vt[hd * HEAD_DIM:(hd + 1) * HEAD_DIM,
                                                        blk * FOX_BLOCK:(blk + 1) * FOX_BLOCK]
            vt_fx_ref[0, blk, r0 + HEAD_DIM:r0 + FOX_VT_ROWS, :] = ones_rows

    cos = cos_ref[...]
    sin = sin_ref[...]
    for c, (_, dilation) in enumerate(DIL_PATTERNS):
        q = _rope(_head_rms(sec(_SEC_Q_DL + c), gm, gain(2)), cos, sin) * QK_SCALE
        k = _rope(_head_rms(sec(_SEC_K_DL + c), gm, gain(3)), cos, sin)
        v = sec(_SEC_V_DL + c)
        for t, val in enumerate((q, k, v)):
            out_ref = dil_refs[3 * c + t]
            if dilation == 1:
                out_ref[0, 0] = val.astype(BF16)
            else:
                for half in range(BRANCH_WIDTH // V7X_LANES):
                    stride_ref[half] = val[:, half * V7X_LANES:(half + 1) * V7X_LANES]
                for r in range(dilation):
                    rows = pl.ds(r, tm // dilation, stride=dilation)
                    out_ref[0, r] = jnp.concatenate(
                        [stride_ref[half, rows, :] for half in range(BRANCH_WIDTH // V7X_LANES)], axis=1).astype(BF16)

    q_mm_ref[0] = (_head_rms(sec(_SEC_Q_MM), gm, gain(4)) * QK_SCALE).astype(BF16)

    for c in range(N_BRANCH):
        z = sec(_SEC_Z[c])
        z_ref[0, :, c * BRANCH_WIDTH:(c + 1) * BRANCH_WIDTH] = z * _sigmoid(z)


def _proj(x, ln_g, w_main, wf, bf, gains, cos_t, sin_t, group_mean, tri, place, place_f, q_bias, tm):
    b, s, _ = x.shape
    bw = BRANCH_WIDTH
    fox_w = HEADS_PER_BRANCH * FOX_HEAD_LANES
    vt_rows = HEADS_PER_BRANCH * FOX_VT_ROWS

    def tok(width, dtype):
        return jax.ShapeDtypeStruct((b, s, width), dtype)

    def tok_spec(width):
        return pl.BlockSpec((1, tm, width), lambda i, j: (i, j, 0))

    def const_spec(shape):
        return pl.BlockSpec(shape, lambda i, j: (0,) * len(shape))

    dil_shapes, dil_specs = [], []
    for _, d in DIL_PATTERNS:
        dil_shapes += [jax.ShapeDtypeStruct((b, d, s // d, bw), BF16)] * 3
        dil_specs += [pl.BlockSpec((1, d, tm // d, bw), lambda i, j: (i, 0, j, 0))] * 3

    out_shape = ([tok(bw, BF16)] * 3 + [tok(fox_w, BF16)] * 2
                 + [jax.ShapeDtypeStruct((b, s // FOX_BLOCK, vt_rows, FOX_BLOCK), BF16)]
                 + dil_shapes + [tok(bw, BF16), tok(N_BRANCH * bw, F32)])
    out_specs = ([tok_spec(bw)] * 3 + [tok_spec(fox_w)] * 2
                 + [pl.BlockSpec((1, tm // FOX_BLOCK, vt_rows, FOX_BLOCK), lambda i, j: (i, j, 0, 0))]
                 + dil_specs + [tok_spec(bw), tok_spec(N_BRANCH * bw)])
    return pl.pallas_call(
        _proj_kernel,
        grid=(b, s // tm),
        in_specs=[
            tok_spec(D_MODEL),
            const_spec((1, D_MODEL)),
            const_spec((D_MODEL, _N_SEC * bw)),
            const_spec((D_MODEL, V7X_LANES)),
            const_spec((1, V7X_LANES)),
            const_spec((8, bw)),
            pl.BlockSpec((tm, bw), lambda i, j: (j, 0)),
            pl.BlockSpec((tm, bw), lambda i, j: (j, 0)),
            const_spec((bw, bw)),
            const_spec((tm, tm)),
            const_spec((bw, fox_w)),
            const_spec((FOX_BIAS_TERMS, V7X_LANES, fox_w)),
            const_spec((1, fox_w)),
        ],
        out_specs=out_specs,
        out_shape=out_shape,
        scratch_shapes=[pltpu.VMEM((bw // V7X_LANES, tm, V7X_LANES), F32), pltpu.VMEM((1, V7X_LANES), F32)],
        compiler_params=pltpu.CompilerParams(dimension_semantics=("arbitrary", "arbitrary"),
                                             vmem_limit_bytes=V7X_VMEM_LIMIT_BYTES),
        name="proj",
    )(x, ln_g, w_main, wf, bf, gains, cos_t, sin_t, group_mean, tri, place, place_f, q_bias)


def _sb_kernel(q_ref, k_ref, v_ref, tri_ref, o_ref, acc_ref, c_ref, *, tq):
    i = pl.program_id(1)
    q = q_ref[0]
    tri = tri_ref[...]
    acc_ref[...] = jnp.zeros_like(acc_ref)
    c_ref[...] = jnp.zeros_like(c_ref)
    row = lax.broadcasted_iota(jnp.int32, (tq, tq), 0)
    col = lax.broadcasted_iota(jnp.int32, (tq, tq), 1)
    delta = row - col

    def cond(state):
        j, c_max = state
        return jnp.logical_and(j >= 0, c_max > SB_UNDERFLOW)

    def body(state):
        j, _ = state
        k0 = pl.multiple_of(j * tq, tq)
        kb = k_ref[0, pl.ds(k0, tq), :]
        vb = v_ref[0, pl.ds(k0, tq), :]
        valid = delta > (j - i) * tq
        heads = range(HEADS_PER_BRANCH)
        cols = [slice(hd * HEAD_DIM, (hd + 1) * HEAD_DIM) for hd in heads]
        z = [jnp.where(valid, _dot_nt(q[:, cols[hd]], kb[:, cols[hd]]), MASKED_LOGIT) for hd in heads]
        t = [jnp.log(1.0 + jnp.exp(-jnp.abs(z[hd]))) for hd in heads]
        log_beta = [jnp.minimum(z[hd], 0.0) - t[hd] for hd in heads]
        log_keep = [-jnp.maximum(z[hd], 0.0) - t[hd] for hd in heads]
        c = [c_ref[hd] for hd in heads]
        later = [_dot(log_keep[hd].astype(BF16), tri) + c[hd] for hd in heads]
        a = [jnp.exp(log_beta[hd] + later[hd]).astype(BF16) for hd in heads]
        pv = [_dot(a[hd], vb[:, cols[hd]]) for hd in heads]
        c_max = jnp.float32(-jnp.inf)
        for hd in heads:
            acc_ref[:, cols[hd]] += pv[hd]
            c_new = c[hd] + jnp.sum(log_keep[hd], axis=-1, keepdims=True)
            c_ref[hd] = c_new
            c_max = jnp.maximum(c_max, jnp.max(c_new))
        return j - 1, c_max

    lax.while_loop(cond, body, (i, jnp.float32(0.0)))
    o_ref[0] = acc_ref[...]


def _sb_attention(q, k, v, tri_strict, tq):
    b, s, w = q.shape
    return pl.pallas_call(
        functools.partial(_sb_kernel, tq=tq),
        grid=(b, s // tq),
        in_specs=[
            pl.BlockSpec((1, tq, w), lambda i, j: (i, j, 0)),
            pl.BlockSpec((1, s, w), lambda i, j: (i, 0, 0)),
            pl.BlockSpec((1, s, w), lambda i, j: (i, 0, 0)),
            pl.BlockSpec((tq, tq), lambda i, j: (0, 0)),
        ],
        out_specs=pl.BlockSpec((1, tq, w), lambda i, j: (i, j, 0)),
        out_shape=jax.ShapeDtypeStruct((b, s, w), F32),
        scratch_shapes=[pltpu.VMEM((tq, w), F32), pltpu.VMEM((HEADS_PER_BRANCH, tq, 1), F32)],
        compiler_params=pltpu.CompilerParams(dimension_semantics=("arbitrary", "arbitrary"),
                                             vmem_limit_bytes=V7X_VMEM_LIMIT_BYTES),
        name="sb_attention",
    )(q, k, v, tri_strict)


def _fox_kernel(q_ref, k_ref, vt_ref, o_ref, acc_ref, m_ref):
    i = pl.program_id(1)
    tq = q_ref.shape[1]
    q = q_ref[0]
    acc_ref[...] = jnp.zeros_like(acc_ref)
    m_ref[...] = jnp.full_like(m_ref, -jnp.inf)

    def step(j, masked):
        k0 = pl.multiple_of(j * FOX_BLOCK, FOX_BLOCK)
        kb = k_ref[0, pl.ds(k0, FOX_BLOCK), :]
        heads = range(HEADS_PER_BRANCH)
        scores = []
        for hd in heads:
            lanes = slice(hd * FOX_HEAD_LANES, (hd + 1) * FOX_HEAD_LANES)
            s = _dot_nt(kb[:, lanes], q[:, lanes])
            if masked:
                key = lax.broadcasted_iota(jnp.int32, s.shape, 0)
                qry = lax.broadcasted_iota(jnp.int32, s.shape, 1)
                s = jnp.where(key <= qry, s, -jnp.inf)
            scores.append(s)
        m_old = [m_ref[hd] for hd in heads]
        m_new = [jnp.maximum(m_old[hd], jnp.max(scores[hd], axis=0, keepdims=True)) for hd in heads]
        probs = [jnp.exp(scores[hd] - m_new[hd]).astype(BF16) for hd in heads]
        pv = [_dot(vt_ref[0, j, hd * FOX_VT_ROWS:(hd + 1) * FOX_VT_ROWS, :], probs[hd]) for hd in heads]
        for hd in heads:
            acc_ref[hd] = jnp.exp(m_old[hd] - m_new[hd]) * acc_ref[hd] + pv[hd]
            m_ref[hd] = m_new[hd]

    def body(j, carry):
        step(j, False)
        return carry

    lax.fori_loop(0, i, body, 0)
    step(i, True)
    for hd in range(HEADS_PER_BRANCH):
        acc = acc_ref[hd]
        o_t = acc[:HEAD_DIM] / acc[HEAD_DIM:HEAD_DIM + 1]
        o_ref[0, :, hd * HEAD_DIM:(hd + 1) * HEAD_DIM] = o_t.T


def _fox_attention(q, k, vt):
    b, s, w = q.shape
    tq = FOX_BLOCK
    n_blk, vt_rows, _ = vt.shape[1:]
    return pl.pallas_call(
        _fox_kernel,
        grid=(b, s // tq),
        in_specs=[
            pl.BlockSpec((1, tq, w), lambda i, j: (i, j, 0)),
            pl.BlockSpec((1, s, w), lambda i, j: (i, 0, 0)),
            pl.BlockSpec((1, n_blk, vt_rows, FOX_BLOCK), lambda i, j: (i, 0, 0, 0)),
        ],
        out_specs=pl.BlockSpec((1, tq, BRANCH_WIDTH), lambda i, j: (i, j, 0)),
        out_shape=jax.ShapeDtypeStruct((b, s, BRANCH_WIDTH), F32),
        scratch_shapes=[pltpu.VMEM((HEADS_PER_BRANCH, FOX_VT_ROWS, tq), F32),
                        pltpu.VMEM((HEADS_PER_BRANCH, 1, tq), F32)],
        compiler_params=pltpu.CompilerParams(dimension_semantics=("arbitrary", "arbitrary"),
                                             vmem_limit_bytes=V7X_VMEM_LIMIT_BYTES),
        name="fox_attention",
    )(q, k, vt)


def _dil_kernel(q_ref, kc_ref, kp_ref, vc_ref, vp_ref, o_ref, lse_ref, stride_ref, *, n_back, dilation):
    span_u = q_ref.shape[2]
    n_blk = span_u // DIL_BLOCK
    qi = lax.broadcasted_iota(jnp.int32, (DIL_BLOCK, 2 * DIL_BLOCK), 0)
    kc = lax.broadcasted_iota(jnp.int32, (DIL_BLOCK, 2 * DIL_BLOCK), 1)
    dist = qi + DIL_BLOCK - kc
    in_window = (dist >= 0) & (dist <= n_back)
    first_key = jnp.where(pl.program_id(1) == 0, DIL_BLOCK, 0)
    in_window_first = in_window & (kc >= first_key)
    for r in range(dilation):
        for ub in range(n_blk):
            rows = slice(ub * DIL_BLOCK, (ub + 1) * DIL_BLOCK)
            q = q_ref[0, r, rows, :]
            if ub == 0:
                k_prev, v_prev = kp_ref[0, r, span_u - DIL_BLOCK:, :], vp_ref[0, r, span_u - DIL_BLOCK:, :]
                valid = in_window_first
            else:
                prev = slice((ub - 1) * DIL_BLOCK, ub * DIL_BLOCK)
                k_prev, v_prev = kc_ref[0, r, prev, :], vc_ref[0, r, prev, :]
                valid = in_window
            kcat = jnp.concatenate([k_prev, kc_ref[0, r, rows, :]], axis=0)
            vcat = jnp.concatenate([v_prev, vc_ref[0, r, rows, :]], axis=0)
            heads = range(HEADS_PER_BRANCH)
            cols = [slice(hd * HEAD_DIM, (hd + 1) * HEAD_DIM) for hd in heads]
            s = [jnp.where(valid, _dot_nt(q[:, cols[hd]], kcat[:, cols[hd]]), -jnp.inf) for hd in heads]
            m = [jnp.max(s[hd], axis=-1, keepdims=True) for hd in heads]
            p = [jnp.exp(s[hd] - m[hd]) for hd in heads]
            denom = [jnp.sum(p[hd], axis=-1, keepdims=True) for hd in heads]
            pv = [_dot(p[hd].astype(BF16), vcat[:, cols[hd]]) for hd in heads]
            outs = [pv[hd] / denom[hd] for hd in heads]
            lses = [jnp.broadcast_to(m[hd] + jnp.log(denom[hd]), (DIL_BLOCK, HEAD_DIM)) for hd in heads]
            results = (jnp.concatenate(outs, axis=1), jnp.concatenate(lses, axis=1))
            if dilation == 1:
                o_ref[0, rows, :], lse_ref[0, rows, :] = results
            else:
                dst = pl.ds(ub * DIL_BLOCK * dilation + r, DIL_BLOCK, stride=dilation)
                for t, val in enumerate(results):
                    for half in range(BRANCH_WIDTH // V7X_LANES):
                        stride_ref[t, half, dst, :] = val[:, half * V7X_LANES:(half + 1) * V7X_LANES]
    if dilation > 1:
        for t, out_ref in enumerate((o_ref, lse_ref)):
            for half in range(BRANCH_WIDTH // V7X_LANES):
                out_ref[0, :, half * V7X_LANES:(half + 1) * V7X_LANES] = stride_ref[t, half]


def _dil_attention(q, k, v, window, dilation):
    b, _, l, w = q.shape
    s = l * dilation
    span = min(DIL_SPAN, s)
    assert s % span == 0 and (span // dilation) % DIL_BLOCK == 0, (s, span, dilation)
    blk = (1, dilation, span // dilation, w)

    def cur(i, j):
        return (i, 0, j, 0)

    def prev(i, j):
        return (i, 0, jnp.maximum(j - 1, 0), 0)

    out = jax.ShapeDtypeStruct((b, s, w), F32)
    return pl.pallas_call(
        functools.partial(_dil_kernel, n_back=window // dilation, dilation=dilation),
        grid=(b, s // span),
        in_specs=[pl.BlockSpec(blk, cur), pl.BlockSpec(blk, cur), pl.BlockSpec(blk, prev),
                  pl.BlockSpec(blk, cur), pl.BlockSpec(blk, prev)],
        out_specs=[pl.BlockSpec((1, span, w), lambda i, j: (i, j, 0))] * 2,
        out_shape=[out, out],
        scratch_shapes=[pltpu.VMEM((2, w // V7X_LANES, span, V7X_LANES), F32)],
        compiler_params=pltpu.CompilerParams(dimension_semantics=("arbitrary", "arbitrary"),
                                             vmem_limit_bytes=V7X_VMEM_LIMIT_BYTES),
        name=f"dil_attention_d{dilation}",
    )(q, k, k, v, v)


def _out_kernel(x_ref, g_ref, o_sb_ref, o_fx_ref, o0_ref, l0_ref, o1_ref, l1_ref, o2_ref, l2_ref,
                q_mm_ref, km_ref, vm_ref, z_ref, wg_ref, wbr_ref, wout_ref, y_ref):
    x = x_ref[0]
    h = _rms_rows(x, g_ref[...]).astype(BF16)

    l0, l1, l2 = l0_ref[0], l1_ref[0], l2_ref[0]
    m = jnp.maximum(jnp.maximum(l0, l1), l2)
    e0, e1, e2 = jnp.exp(l0 - m), jnp.exp(l1 - m), jnp.exp(l2 - m)
    o_dl = (e0 * o0_ref[0] + e1 * o1_ref[0] + e2 * o2_ref[0]) / (e0 + e1 + e2)

    q = q_mm_ref[0]
    km = km_ref[0]
    vm = vm_ref[0]
    heads = range(HEADS_PER_BRANCH)
    cols = [slice(hd * HEAD_DIM, (hd + 1) * HEAD_DIM) for hd in heads]
    s = [_dot_nt(q[:, cols[hd]], km[:, cols[hd]]) for hd in heads]
    p = [jnp.exp(s[hd] - jnp.max(s[hd], axis=-1, keepdims=True)) for hd in heads]
    pv = [_dot(p[hd].astype(BF16), vm[:, cols[hd]]) for hd in heads]
    o_mm = jnp.concatenate([pv[hd] / jnp.sum(p[hd], axis=-1, keepdims=True) for hd in heads], axis=1)

    merged = None
    for br, o in enumerate((o_sb_ref[0], o_fx_ref[0], o_dl, o_mm)):
        gated = (o * z_ref[0, :, br * BRANCH_WIDTH:(br + 1) * BRANCH_WIDTH]).astype(BF16)
        y = _dot(gated, wbr_ref[br])
        gate = _sigmoid(_dot(h, wg_ref[:, br * D_MODEL:(br + 1) * D_MODEL]))
        merged = gate * y if merged is None else merged + gate * y
    y_ref[0] = x + _dot(merged.astype(BF16), wout_ref[...])


def _out_layer(x, ln_g, o_sb, o_fx, dil, q_mm, km, vm, z, w_gate, w_br, w_out, tm):
    b, s, _ = x.shape
    bw = BRANCH_WIDTH
    n_mem = km.shape[1]

    def tok_spec(width):
        return pl.BlockSpec((1, tm, width), lambda i, j: (i, j, 0))

    def const_spec(shape):
        return pl.BlockSpec(shape, lambda i, j: (0,) * len(shape))

    mem_spec = pl.BlockSpec((1, n_mem, bw), lambda i, j: (i, 0, 0))
    dil_args = [t for pair in dil for t in pair]
    return pl.pallas_call(
        _out_kernel,
        grid=(b, s // tm),
        in_specs=([tok_spec(D_MODEL), const_spec((1, D_MODEL)), tok_spec(bw), tok_spec(bw)]
                  + [tok_spec(bw)] * (2 * N_DIL)
                  + [tok_spec(bw), mem_spec, mem_spec, tok_spec(N_BRANCH * bw),
                     const_spec((D_MODEL, N_BRANCH * D_MODEL)), const_spec((N_BRANCH, bw, D_MODEL)),
                     const_spec((D_MODEL, D_MODEL))]),
        out_specs=tok_spec(D_MODEL),
        out_shape=jax.ShapeDtypeStruct((b, s, D_MODEL), F32),
        compiler_params=pltpu.CompilerParams(dimension_semantics=("arbitrary", "arbitrary"),
                                             vmem_limit_bytes=V7X_VMEM_LIMIT_BYTES),
        name="out_layer",
    )(x, ln_g, o_sb, o_fx, *dil_args, q_mm, km, vm, z, w_gate, w_br, w_out)


def _tile(n, pref):
    t = min(pref, n)
    assert n % t == 0, (n, t)
    return t


def _rope_tables(s):
    half = HEAD_DIM // 2
    inv = ROPE_THETA ** (-jnp.arange(half, dtype=F32) / half)
    ang = jnp.arange(s, dtype=F32)[:, None] * inv[None, :]
    cos, sin = jnp.cos(ang), jnp.sin(ang)
    cos_t = jnp.tile(jnp.concatenate([cos, cos], axis=1), (1, HEADS_PER_BRANCH))
    sin_t = jnp.tile(jnp.concatenate([-sin, sin], axis=1), (1, HEADS_PER_BRANCH))
    return cos_t, sin_t


def _fox_placement():
    fox_w = HEADS_PER_BRANCH * FOX_HEAD_LANES
    place = np.zeros((BRANCH_WIDTH, fox_w), np.float32)
    place_f = np.zeros((FOX_BIAS_TERMS, V7X_LANES, fox_w), np.float32)
    q_bias = np.zeros((1, fox_w), np.float32)
    for hd in range(HEADS_PER_BRANCH):
        for d in range(HEAD_DIM):
            place[hd * HEAD_DIM + d, hd * FOX_HEAD_LANES + d] = 1.0
        for t in range(FOX_BIAS_TERMS):
            place_f[t, hd, hd * FOX_HEAD_LANES + HEAD_DIM + t] = 1.0
            q_bias[0, hd * FOX_HEAD_LANES + HEAD_DIM + t] = -1.0
    return jnp.asarray(place, BF16), jnp.asarray(place_f, BF16), jnp.asarray(q_bias, F32)


def kernel(x, mem, ln_gain, mem_ln_gain, qk_gain, w_in, b_forget, w_mem_kv, w_br_sb, w_br_fox, w_br_dil,
           w_br_mem, w_out):
    depth = w_in.shape[0]
    b, s, _ = x.shape
    bw = BRANCH_WIDTH
    tm_proj = _tile(s, 512)
    tq = _tile(s, 256)
    tm_out = _tile(s, 256)
    assert tm_proj % FOX_BLOCK == 0 and s % FOX_BLOCK == 0

    head_id = np.arange(bw) // HEAD_DIM
    group_mean = jnp.asarray((head_id[:, None] == head_id[None, :]) / HEAD_DIM, BF16)
    tri_prefix = jnp.asarray(np.arange(tm_proj)[:, None] >= np.arange(tm_proj)[None, :], BF16)
    tri_later = jnp.asarray(np.arange(tq)[:, None] > np.arange(tq)[None, :], BF16)
    cos_t, sin_t = _rope_tables(s)
    place, place_f, q_bias = _fox_placement()

    gains = jnp.tile(qk_gain, (1, 1, HEADS_PER_BRANCH))
    km_all, vm_all = _mem_kv(mem, mem_ln_gain[:, None, :], w_mem_kv.astype(BF16), gains[:, 5:6, :], group_mean)

    assert w_in.shape[2] == (_N_SEC + N_BRANCH * D_MODEL // bw) * bw + HEADS_PER_BRANCH, w_in.shape
    main_chunks = _N_SEC * bw // PREP_CHUNK
    w_main_all = _prep_weights(w_in, 0, main_chunks)
    w_gate_all = _prep_weights(w_in, main_chunks, N_BRANCH * D_MODEL // PREP_CHUNK)
    wf_all = jnp.pad(w_in[:, :, _FORGET_COL:_FORGET_COL + HEADS_PER_BRANCH],
                     ((0, 0), (0, 0), (0, V7X_LANES - HEADS_PER_BRANCH))).astype(BF16)

    for l in range(depth):
        w_main, w_gate, wf = w_main_all[l], w_gate_all[l], wf_all[l]
        bf = jnp.pad(b_forget[l], (0, V7X_LANES - HEADS_PER_BRANCH))[None, :]
        gains_l = jnp.pad(gains[l], ((0, 2), (0, 0)))
        ln_g = ln_gain[l][None, :]

        outs = _proj(x, ln_g, w_main, wf, bf, gains_l, cos_t, sin_t, group_mean, tri_prefix, place, place_f,
                     q_bias, tm_proj)
        q_sb, k_sb, v_sb, q_fx, k_fx, vt_fx = outs[:6]
        dil_qkv = outs[6:6 + 3 * N_DIL]
        q_mm, z = outs[6 + 3 * N_DIL:]

        o_sb = _sb_attention(q_sb, k_sb, v_sb, tri_later, tq)
        o_fx = _fox_attention(q_fx, k_fx, vt_fx)
        dil = [_dil_attention(*dil_qkv[3 * g:3 * g + 3], window, dilation)
               for g, (window, dilation) in enumerate(DIL_PATTERNS)]
        w_br = jnp.stack([w_br_sb[l], w_br_fox[l], w_br_dil[l], w_br_mem[l]]).astype(BF16)
        x = _out_layer(x, ln_g, o_sb, o_fx, dil, q_mm, km_all[l], vm_all[l], z, w_gate, w_br,
                       w_out[l].astype(BF16), tm_out)
    return x
```

```python
import functools

import jax
import jax.numpy as jnp
import numpy as np
from jax import lax
from jax.experimental import pallas as pl
from jax.experimental.pallas import tpu as pltpu

F32 = jnp.float32
BF16 = jnp.bfloat16

D_MODEL = 1024
HEAD_DIM = 64
HEADS_PER_BRANCH = 4
BRANCH_WIDTH = HEADS_PER_BRANCH * HEAD_DIM
DIL_PATTERNS = ((128, 1), (512, 4), (2048, 16))
N_DIL = len(DIL_PATTERNS)
N_BRANCH = 4
DIL_BLOCK = 128
DIL_SPAN = 2048
ROPE_THETA = 10000.0
EPS = 1e-6
MASKED_LOGIT = -1e4
QK_SCALE = HEAD_DIM ** -0.5
SB_UNDERFLOW = -105.0
V7X_VMEM_LIMIT_BYTES = 56 * 1024 * 1024
V7X_LANES = 128
V7X_BF16_SUBLANES = 16
FOX_HEAD_LANES = V7X_LANES
FOX_BIAS_TERMS = 3
FOX_LANE_KEY_F = HEAD_DIM
FOX_LANE_QRY_F = HEAD_DIM + FOX_BIAS_TERMS
FOX_LANE_BOUND = HEAD_DIM + 2 * FOX_BIAS_TERMS
FOX_BOUND_MARGIN = 1.02
FOX_SAFE_BOUND = 30.0
FOX_VT_ROWS = HEAD_DIM + V7X_BF16_SUBLANES
FOX_BLOCK = 512

_NT = (((1,), (1,)), ((), ()))


def _dot(a, b):
    return jnp.dot(a, b, preferred_element_type=F32)


def _dot_nt(a, b):
    return lax.dot_general(a, b, _NT, preferred_element_type=F32)


def _bf16_terms(x, terms):
    out = []
    r = x
    for t in range(terms):
        p = r.astype(BF16)
        out.append(p)
        if t + 1 < terms:
            r = r - p.astype(F32)
    return out


def _rms_rows(x, gain):
    ms = jnp.mean(x * x, axis=-1, keepdims=True)
    return x * lax.rsqrt(ms + EPS) * gain


def _head_rms(x, group_mean, gain):
    ms = sum(_dot(p, group_mean) for p in _bf16_terms(x * x, 2))
    return x * lax.rsqrt(ms + EPS) * gain


def _sigmoid(x):
    return 1.0 / (1.0 + jnp.exp(-x))


def _log_sigmoid(x):
    return jnp.minimum(x, 0.0) - jnp.log(1.0 + jnp.exp(-jnp.abs(x)))


def _rope(x, cos, sin_signed):
    lane = lax.broadcasted_iota(jnp.int32, x.shape, 1)
    first_half = (lane & (HEAD_DIM - 1)) < (HEAD_DIM // 2)
    w = x.shape[1]
    partner = jnp.where(first_half, pltpu.roll(x, w - HEAD_DIM // 2, 1), pltpu.roll(x, HEAD_DIM // 2, 1))
    return x * cos + partner * sin_signed


def _mem_kv_kernel(mem_ref, g_ref, w_ref, gk_ref, gm_ref, km_ref, vm_ref):
    h = _rms_rows(mem_ref[0], g_ref[0]).astype(BF16)
    kv = _dot(h, w_ref[0])
    km_ref[0, 0] = _head_rms(kv[:, :BRANCH_WIDTH], gm_ref[...], gk_ref[0]).astype(BF16)
    vm_ref[0, 0] = kv[:, BRANCH_WIDTH:].astype(BF16)


def _mem_kv(mem, mem_ln_gain, w_mem_kv, gain_k, group_mean):
    depth = w_mem_kv.shape[0]
    b, n_mem, _ = mem.shape
    out = jax.ShapeDtypeStruct((depth, b, n_mem, BRANCH_WIDTH), BF16)
    return pl.pallas_call(
        _mem_kv_kernel,
        grid=(depth, b),
        in_specs=[
            pl.BlockSpec((1, n_mem, D_MODEL), lambda l, i: (i, 0, 0)),
            pl.BlockSpec((1, 1, D_MODEL), lambda l, i: (l, 0, 0)),
            pl.BlockSpec((1, D_MODEL, 2 * BRANCH_WIDTH), lambda l, i: (l, 0, 0)),
            pl.BlockSpec((1, 1, BRANCH_WIDTH), lambda l, i: (l, 0, 0)),
            pl.BlockSpec((BRANCH_WIDTH, BRANCH_WIDTH), lambda l, i: (0, 0)),
        ],
        out_specs=[pl.BlockSpec((1, 1, n_mem, BRANCH_WIDTH), lambda l, i: (l, i, 0, 0))] * 2,
        out_shape=[out, out],
        compiler_params=pltpu.CompilerParams(dimension_semantics=("arbitrary", "arbitrary")),
        name="mem_kv",
    )(mem, mem_ln_gain, w_mem_kv, gain_k, group_mean)


_SEC_Q_SB, _SEC_K_SB, _SEC_V_SB, _SEC_Z_SB = 0, 1, 2, 3
_SEC_Q_FX, _SEC_K_FX, _SEC_V_FX, _SEC_Z_FX = 4, 5, 6, 7
_SEC_Q_DL, _SEC_K_DL, _SEC_V_DL, _SEC_Z_DL = 8, 11, 14, 17
_SEC_Q_MM, _SEC_Z_MM = 18, 19
_SEC_Z = (_SEC_Z_SB, _SEC_Z_FX, _SEC_Z_DL, _SEC_Z_MM)
_N_SEC = 20
_FORGET_COL = 8 * BRANCH_WIDTH
PREP_CHUNK = 512


def _prep_kernel(a_ref, b_ref, o_ref, *, first_chunk):
    shifted = jnp.concatenate([a_ref[0], b_ref[0]], axis=1)[:, HEADS_PER_BRANCH:HEADS_PER_BRANCH + PREP_CHUNK]
    if first_chunk * PREP_CHUNK >= _FORGET_COL:
        o_ref[0] = shifted.astype(BF16)
    else:
        before_forget = (pl.program_id(1) + first_chunk) * PREP_CHUNK < _FORGET_COL
        o_ref[0] = jnp.where(before_forget, a_ref[0], shifted).astype(BF16)


def _prep_weights(w_in, first_chunk, n_chunks):
    depth, d_model, _ = w_in.shape
    tail = PREP_CHUNK // V7X_LANES
    return pl.pallas_call(
        functools.partial(_prep_kernel, first_chunk=first_chunk),
        grid=(depth, n_chunks),
        in_specs=[pl.BlockSpec((1, d_model, PREP_CHUNK), lambda l, c: (l, 0, c + first_chunk)),
                  pl.BlockSpec((1, d_model, V7X_LANES), lambda l, c: (l, 0, tail * (c + first_chunk + 1)))],
        out_specs=pl.BlockSpec((1, d_model, PREP_CHUNK), lambda l, c: (l, 0, c)),
        out_shape=jax.ShapeDtypeStruct((depth, d_model, n_chunks * PREP_CHUNK), BF16),
        compiler_params=pltpu.CompilerParams(dimension_semantics=("arbitrary", "arbitrary")),
        name="prep_weights",
    )(w_in, w_in)


def _proj_kernel(x_ref, g_ref, w_ref, wf_ref, bf_ref, gains_ref, cos_ref, sin_ref, gm_ref, tri_ref,
                 place_ref, place_f_ref, q_const_ref, k_const_ref,
                 q_sb_ref, k_sb_ref, v_sb_ref, q_fx_ref, k_fx_ref, vt_fx_ref, *rest):
    dil_refs = rest[:3 * N_DIL]
    q_mm_ref, z_ref, stride_ref, carry_ref = rest[3 * N_DIL:]
    tm = x_ref.shape[1]
    h = _rms_rows(x_ref[0], g_ref[...]).astype(BF16)
    gm = gm_ref[...]

    def sec(i):
        return _dot(h, w_ref[:, i * BRANCH_WIDTH:(i + 1) * BRANCH_WIDTH])

    def gain(i):
        return gains_ref[i:i + 1, :]

    q_sb_ref[0] = (sec(_SEC_Q_SB) * QK_SCALE).astype(BF16)
    k_sb_ref[0] = sec(_SEC_K_SB).astype(BF16)
    v_sb_ref[0] = sec(_SEC_V_SB).astype(BF16)

    @pl.when(pl.program_id(1) == 0)
    def _():
        carry_ref[...] = jnp.zeros_like(carry_ref)

    log_f = _log_sigmoid(_dot(h, wf_ref[...]) + bf_ref[...])
    tri = tri_ref[...]
    fcum = sum(_dot(tri, p) for p in _bf16_terms(log_f, 3)) + carry_ref[...]
    carry_ref[...] = fcum[tm - 1:tm, :]
    place = place_ref[...]
    q_fx = (_head_rms(sec(_SEC_Q_FX), gm, gain(0)) * QK_SCALE).astype(BF16)
    k_fx = _head_rms(sec(_SEC_K_FX), gm, gain(1)).astype(BF16)
    f_lanes = sum(_dot(p, place_f_ref[t]) for t, p in enumerate(_bf16_terms(fcum, FOX_BIAS_TERMS)))
    lane = lax.broadcasted_iota(jnp.int32, f_lanes.shape, 1) & (FOX_HEAD_LANES - 1)
    key_f = (lane >= FOX_LANE_KEY_F) & (lane < FOX_LANE_QRY_F)
    q_fx_ref[0] = (_dot(q_fx, place) + jnp.where(key_f, 0.0, f_lanes) + q_const_ref[...]).astype(BF16)
    k_fx_ref[0] = (_dot(k_fx, place) + jnp.where(key_f, f_lanes, 0.0) + k_const_ref[...]).astype(BF16)
    vt = sec(_SEC_V_FX).T.astype(BF16)
    ones_rows = (lax.broadcasted_iota(jnp.int32, (V7X_BF16_SUBLANES, FOX_BLOCK), 0) == 0).astype(BF16)
    for blk in range(tm // FOX_BLOCK):
        for hd in range(HEADS_PER_BRANCH):
            r0 = hd * FOX_VT_ROWS
            vt_fx_ref[0, blk, r0:r0 + HEAD_DIM, :] = vt[hd * HEAD_DIM:(hd + 1) * HEAD_DIM,
                                                        blk * FOX_BLOCK:(blk + 1) * FOX_BLOCK]
            vt_fx_ref[0, blk, r0 + HEAD_DIM:r0 + FOX_VT_ROWS, :] = ones_rows

    cos = cos_ref[...]
    sin = sin_ref[...]
    for c, (_, dilation) in enumerate(DIL_PATTERNS):
        q = _rope(_head_rms(sec(_SEC_Q_DL + c), gm, gain(2)), cos, sin) * QK_SCALE
        k = _rope(_head_rms(sec(_SEC_K_DL + c), gm, gain(3)), cos, sin)
        v = sec(_SEC_V_DL + c)
        for t, val in enumerate((q, k, v)):
            out_ref = dil_refs[3 * c + t]
            if dilation == 1:
                out_ref[0, 0] = val.astype(BF16)
            else:
                for half in range(BRANCH_WIDTH // V7X_LANES):
                    stride_ref[half] = val[:, half * V7X_LANES:(half + 1) * V7X_LANES]
                for r in range(dilation):
                    rows = pl.ds(r, tm // dilation, stride=dilation)
                    out_ref[0, r] = jnp.concatenate(
                        [stride_ref[half, rows, :] for half in range(BRANCH_WIDTH // V7X_LANES)], axis=1).astype(BF16)

    q_mm_ref[0] = (_head_rms(sec(_SEC_Q_MM), gm, gain(4)) * QK_SCALE).astype(BF16)

    for c in range(N_BRANCH):
        z = sec(_SEC_Z[c])
        z_ref[0, :, c * BRANCH_WIDTH:(c + 1) * BRANCH_WIDTH] = z * _sigmoid(z)


def _proj(x, ln_g, w_main, wf, bf, gains, cos_t, sin_t, group_mean, tri, place, place_f, q_const, k_const, tm):
    b, s, _ = x.shape
    bw = BRANCH_WIDTH
    fox_w = HEADS_PER_BRANCH * FOX_HEAD_LANES
    vt_rows = HEADS_PER_BRANCH * FOX_VT_ROWS

    def tok(width, dtype):
        return jax.ShapeDtypeStruct((b, s, width), dtype)

    def tok_spec(width):
        return pl.BlockSpec((1, tm, width), lambda i, j: (i, j, 0))

    def const_spec(shape):
        return pl.BlockSpec(shape, lambda i, j: (0,) * len(shape))

    dil_shapes, dil_specs = [], []
    for _, d in DIL_PATTERNS:
        dil_shapes += [jax.ShapeDtypeStruct((b, d, s // d, bw), BF16)] * 3
        dil_specs += [pl.BlockSpec((1, d, tm // d, bw), lambda i, j: (i, 0, j, 0))] * 3

    out_shape = ([tok(bw, BF16)] * 3 + [tok(fox_w, BF16)] * 2
                 + [jax.ShapeDtypeStruct((b, s // FOX_BLOCK, vt_rows, FOX_BLOCK), BF16)]
                 + dil_shapes + [tok(bw, BF16), tok(N_BRANCH * bw, F32)])
    out_specs = ([tok_spec(bw)] * 3 + [tok_spec(fox_w)] * 2
                 + [pl.BlockSpec((1, tm // FOX_BLOCK, vt_rows, FOX_BLOCK), lambda i, j: (i, j, 0, 0))]
                 + dil_specs + [tok_spec(bw), tok_spec(N_BRANCH * bw)])
    return pl.pallas_call(
        _proj_kernel,
        grid=(b, s // tm),
        in_specs=[
            tok_spec(D_MODEL),
            const_spec((1, D_MODEL)),
            const_spec((D_MODEL, _N_SEC * bw)),
            const_spec((D_MODEL, V7X_LANES)),
            const_spec((1, V7X_LANES)),
            const_spec((8, bw)),
            pl.BlockSpec((tm, bw), lambda i, j: (j, 0)),
            pl.BlockSpec((tm, bw), lambda i, j: (j, 0)),
            const_spec((bw, bw)),
            const_spec((tm, tm)),
            const_spec((bw, fox_w)),
            const_spec((FOX_BIAS_TERMS, V7X_LANES, fox_w)),
            const_spec((1, fox_w)),
            const_spec((1, fox_w)),
        ],
        out_specs=out_specs,
        out_shape=out_shape,
        scratch_shapes=[pltpu.VMEM((bw // V7X_LANES, tm, V7X_LANES), F32), pltpu.VMEM((1, V7X_LANES), F32)],
        compiler_params=pltpu.CompilerParams(dimension_semantics=("arbitrary", "arbitrary"),
                                             vmem_limit_bytes=V7X_VMEM_LIMIT_BYTES),
        name="proj",
    )(x, ln_g, w_main, wf, bf, gains, cos_t, sin_t, group_mean, tri, place, place_f, q_const, k_const)


def _sb_kernel(q_ref, k_ref, v_ref, tri_ref, o_ref, acc_ref, c_ref, *, tq):
    i = pl.program_id(1)
    q = q_ref[0]
    tri = tri_ref[...]
    acc_ref[...] = jnp.zeros_like(acc_ref)
    c_ref[...] = jnp.zeros_like(c_ref)
    row = lax.broadcasted_iota(jnp.int32, (tq, tq), 0)
    col = lax.broadcasted_iota(jnp.int32, (tq, tq), 1)
    delta = row - col

    def cond(state):
        j, c_max = state
        return jnp.logical_and(j >= 0, c_max > SB_UNDERFLOW)

    def body(state):
        j, _ = state
        k0 = pl.multiple_of(j * tq, tq)
        kb = k_ref[0, pl.ds(k0, tq), :]
        vb = v_ref[0, pl.ds(k0, tq), :]
        valid = delta > (j - i) * tq
        heads = range(HEADS_PER_BRANCH)
        cols = [slice(hd * HEAD_DIM, (hd + 1) * HEAD_DIM) for hd in heads]
        z = [jnp.where(valid, _dot_nt(q[:, cols[hd]], kb[:, cols[hd]]), MASKED_LOGIT) for hd in heads]
        t = [jnp.log(1.0 + jnp.exp(-jnp.abs(z[hd]))) for hd in heads]
        log_beta = [jnp.minimum(z[hd], 0.0) - t[hd] for hd in heads]
        log_keep = [-jnp.maximum(z[hd], 0.0) - t[hd] for hd in heads]
        c = [c_ref[hd] for hd in heads]
        later = [_dot(log_keep[hd].astype(BF16), tri) + c[hd] for hd in heads]
        a = [jnp.exp(log_beta[hd] + later[hd]).astype(BF16) for hd in heads]
        pv = [_dot(a[hd], vb[:, cols[hd]]) for hd in heads]
        c_max = jnp.float32(-jnp.inf)
        for hd in heads:
            acc_ref[:, cols[hd]] += pv[hd]
            c_new = c[hd] + jnp.sum(log_keep[hd], axis=-1, keepdims=True)
            c_ref[hd] = c_new
            c_max = jnp.maximum(c_max, jnp.max(c_new))
        return j - 1, c_max

    lax.while_loop(cond, body, (i, jnp.float32(0.0)))
    o_ref[0] = acc_ref[...]


def _sb_attention(q, k, v, tri_strict, tq):
    b, s, w = q.shape
    return pl.pallas_call(
        functools.partial(_sb_kernel, tq=tq),
        grid=(b, s // tq),
        in_specs=[
            pl.BlockSpec((1, tq, w), lambda i, j: (i, j, 0)),
            pl.BlockSpec((1, s, w), lambda i, j: (i, 0, 0)),
            pl.BlockSpec((1, s, w), lambda i, j: (i, 0, 0)),
            pl.BlockSpec((tq, tq), lambda i, j: (0, 0)),
        ],
        out_specs=pl.BlockSpec((1, tq, w), lambda i, j: (i, j, 0)),
        out_shape=jax.ShapeDtypeStruct((b, s, w), F32),
        scratch_shapes=[pltpu.VMEM((tq, w), F32), pltpu.VMEM((HEADS_PER_BRANCH, tq, 1), F32)],
        compiler_params=pltpu.CompilerParams(dimension_semantics=("arbitrary", "arbitrary"),
                                             vmem_limit_bytes=V7X_VMEM_LIMIT_BYTES),
        name="sb_attention",
    )(q, k, v, tri_strict)


def _fox_scores(k_ref, q, j, masked):
    k0 = pl.multiple_of(j * FOX_BLOCK, FOX_BLOCK)
    kb = k_ref[0, pl.ds(k0, FOX_BLOCK), :]
    scores = []
    for hd in range(HEADS_PER_BRANCH):
        lanes = slice(hd * FOX_HEAD_LANES, (hd + 1) * FOX_HEAD_LANES)
        s = _dot_nt(kb[:, lanes], q[:, lanes])
        if masked:
            key = lax.broadcasted_iota(jnp.int32, s.shape, 0)
            qry = lax.broadcasted_iota(jnp.int32, s.shape, 1)
            s = jnp.where(key <= qry, s, -jnp.inf)
        scores.append(s)
    return scores


def _fox_values(vt_ref, j, hd):
    return vt_ref[0, j, hd * FOX_VT_ROWS:(hd + 1) * FOX_VT_ROWS, :]


def _fox_finish(acc_ref, o_ref):
    for hd in range(HEADS_PER_BRANCH):
        acc = acc_ref[hd]
        o_t = acc[:HEAD_DIM] / acc[HEAD_DIM:HEAD_DIM + 1]
        o_ref[0, :, hd * HEAD_DIM:(hd + 1) * HEAD_DIM] = o_t.T


def _fox_bounded_kernel(q_ref, k_ref, vt_ref, o_ref, acc_ref):
    i = pl.program_id(1)
    q = q_ref[0]
    heads = range(HEADS_PER_BRANCH)
    acc_ref[...] = jnp.zeros_like(acc_ref)

    def step(j, masked):
        scores = _fox_scores(k_ref, q, j, masked)
        probs = [jnp.exp(scores[hd]).astype(BF16) for hd in heads]
        pv = [_dot(_fox_values(vt_ref, j, hd), probs[hd]) for hd in heads]
        for hd in heads:
            acc_ref[hd] += pv[hd]

    def body(j, carry):
        step(j, False)
        return carry

    lax.fori_loop(0, i, body, 0)
    step(i, True)
    _fox_finish(acc_ref, o_ref)


def _fox_online_kernel(q_ref, k_ref, vt_ref, o_ref, acc_ref, m_ref):
    i = pl.program_id(1)
    q = q_ref[0]
    heads = range(HEADS_PER_BRANCH)
    acc_ref[...] = jnp.zeros_like(acc_ref)
    m_ref[...] = jnp.full_like(m_ref, -jnp.inf)

    def step(j, masked):
        scores = _fox_scores(k_ref, q, j, masked)
        m_old = [m_ref[hd] for hd in heads]
        m_new = [jnp.maximum(m_old[hd], jnp.max(scores[hd], axis=0, keepdims=True)) for hd in heads]
        probs = [jnp.exp(scores[hd] - m_new[hd]).astype(BF16) for hd in heads]
        pv = [_dot(_fox_values(vt_ref, j, hd), probs[hd]) for hd in heads]
        for hd in heads:
            acc_ref[hd] = jnp.exp(m_old[hd] - m_new[hd]) * acc_ref[hd] + pv[hd]
            m_ref[hd] = m_new[hd]

    def body(j, carry):
        step(j, False)
        return carry

    lax.fori_loop(0, i, body, 0)
    step(i, True)
    _fox_finish(acc_ref, o_ref)


def _fox_attention(q, k, vt, bounded):
    b, s, w = q.shape
    tq = FOX_BLOCK
    n_blk, vt_rows, _ = vt.shape[1:]
    scratch = [pltpu.VMEM((HEADS_PER_BRANCH, FOX_VT_ROWS, tq), F32)]
    if not bounded:
        scratch.append(pltpu.VMEM((HEADS_PER_BRANCH, 1, tq), F32))
    return pl.pallas_call(
        _fox_bounded_kernel if bounded else _fox_online_kernel,
        grid=(b, s // tq),
        in_specs=[
            pl.BlockSpec((1, tq, w), lambda i, j: (i, j, 0)),
            pl.BlockSpec((1, s, w), lambda i, j: (i, 0, 0)),
            pl.BlockSpec((1, n_blk, vt_rows, FOX_BLOCK), lambda i, j: (i, 0, 0, 0)),
        ],
        out_specs=pl.BlockSpec((1, tq, BRANCH_WIDTH), lambda i, j: (i, j, 0)),
        out_shape=jax.ShapeDtypeStruct((b, s, BRANCH_WIDTH), F32),
        scratch_shapes=scratch,
        compiler_params=pltpu.CompilerParams(dimension_semantics=("arbitrary", "arbitrary"),
                                             vmem_limit_bytes=V7X_VMEM_LIMIT_BYTES),
        name="fox_attention_bounded" if bounded else "fox_attention_online",
    )(q, k, vt)


def _dil_kernel(q_ref, kc_ref, kp_ref, vc_ref, vp_ref, o_ref, lse_ref, stride_ref, *, n_back, dilation):
    span_u = q_ref.shape[2]
    n_blk = span_u // DIL_BLOCK
    heads = range(HEADS_PER_BRANCH)
    heads_per_tile = V7X_LANES // HEAD_DIM
    n_tiles = BRANCH_WIDTH // V7X_LANES
    key = lax.broadcasted_iota(jnp.int32, (2 * DIL_BLOCK, DIL_BLOCK), 0)
    qry = lax.broadcasted_iota(jnp.int32, (2 * DIL_BLOCK, DIL_BLOCK), 1)
    dist = qry + DIL_BLOCK - key
    in_window = (dist >= 0) & (dist <= n_back)
    first_key = jnp.where(pl.program_id(1) == 0, DIL_BLOCK, 0)
    in_window_first = in_window & (key >= first_key)
    lane = lax.broadcasted_iota(jnp.int32, (DIL_BLOCK, V7X_LANES), 1)
    head_lanes = [(lane >= i * HEAD_DIM) & (lane < (i + 1) * HEAD_DIM) for i in range(heads_per_tile)]

    def transposed(v):
        return v.astype(F32).T.astype(BF16)

    for r in range(dilation):
        vt_prev = transposed(vp_ref[0, r, span_u - DIL_BLOCK:, :])
        for ub in range(n_blk):
            rows = slice(ub * DIL_BLOCK, (ub + 1) * DIL_BLOCK)
            q = q_ref[0, r, rows, :]
            k_prev = (kp_ref[0, r, span_u - DIL_BLOCK:, :] if ub == 0
                      else kc_ref[0, r, (ub - 1) * DIL_BLOCK:ub * DIL_BLOCK, :])
            kcat = jnp.concatenate([k_prev, kc_ref[0, r, rows, :]], axis=0)
            vt_own = transposed(vc_ref[0, r, rows, :])
            valid = in_window_first if ub == 0 else in_window
            s = []
            for hd in heads:
                tile = slice(hd // heads_per_tile * V7X_LANES, (hd // heads_per_tile + 1) * V7X_LANES)
                q_hd = jnp.where(head_lanes[hd % heads_per_tile], q[:, tile], jnp.zeros_like(q[:, tile]))
                s.append(jnp.where(valid, _dot_nt(kcat[:, tile], q_hd), -jnp.inf))
            m = [jnp.max(s[hd], axis=0, keepdims=True) for hd in heads]
            p = [jnp.exp(s[hd] - m[hd]) for hd in heads]
            denom = [jnp.sum(p[hd], axis=0, keepdims=True) for hd in heads]
            pb = [p[hd].astype(BF16) for hd in heads]
            dims = [slice(hd * HEAD_DIM, (hd + 1) * HEAD_DIM) for hd in heads]
            pv = [_dot(vt_prev[dims[hd], :], pb[hd][:DIL_BLOCK]) + _dot(vt_own[dims[hd], :], pb[hd][DIL_BLOCK:])
                  for hd in heads]
            outs = [pv[hd] / denom[hd] for hd in heads]
            lses = [jnp.broadcast_to(m[hd] + jnp.log(denom[hd]), (HEAD_DIM, DIL_BLOCK)) for hd in heads]
            vt_prev = vt_own
            for t, (vals, out_ref) in enumerate(((outs, o_ref), (lses, lse_ref))):
                for tile in range(n_tiles):
                    pair = vals[tile * heads_per_tile:(tile + 1) * heads_per_tile]
                    val = jnp.concatenate(pair, axis=0).T
                    if dilation == 1:
                        out_ref[0, rows, tile * V7X_LANES:(tile + 1) * V7X_LANES] = val
                    else:
                        dst = pl.ds(ub * DIL_BLOCK * dilation + r, DIL_BLOCK, stride=dilation)
                        stride_ref[t, tile, dst, :] = val
    if dilation > 1:
        for t, out_ref in enumerate((o_ref, lse_ref)):
            for tile in range(n_tiles):
                out_ref[0, :, tile * V7X_LANES:(tile + 1) * V7X_LANES] = stride_ref[t, tile]


def _dil_attention(q, k, v, window, dilation):
    b, _, l, w = q.shape
    s = l * dilation
    span = min(DIL_SPAN, s)
    assert s % span == 0 and (span // dilation) % DIL_BLOCK == 0, (s, span, dilation)
    blk = (1, dilation, span // dilation, w)

    def cur(i, j):
        return (i, 0, j, 0)

    def prev(i, j):
        return (i, 0, jnp.maximum(j - 1, 0), 0)

    out = jax.ShapeDtypeStruct((b, s, w), F32)
    return pl.pallas_call(
        functools.partial(_dil_kernel, n_back=window // dilation, dilation=dilation),
        grid=(b, s // span),
        in_specs=[pl.BlockSpec(blk, cur), pl.BlockSpec(blk, cur), pl.BlockSpec(blk, prev),
                  pl.BlockSpec(blk, cur), pl.BlockSpec(blk, prev)],
        out_specs=[pl.BlockSpec((1, span, w), lambda i, j: (i, j, 0))] * 2,
        out_shape=[out, out],
        scratch_shapes=[pltpu.VMEM((2, w // V7X_LANES, span, V7X_LANES), F32)],
        compiler_params=pltpu.CompilerParams(dimension_semantics=("arbitrary", "arbitrary"),
                                             vmem_limit_bytes=V7X_VMEM_LIMIT_BYTES),
        name=f"dil_attention_d{dilation}",
    )(q, k, k, v, v)


def _out_kernel(x_ref, g_ref, o_sb_ref, o_fx_ref, o0_ref, l0_ref, o1_ref, l1_ref, o2_ref, l2_ref,
                q_mm_ref, km_ref, vm_ref, z_ref, wg_ref, wbr_ref, wout_ref, y_ref):
    x = x_ref[0]
    h = _rms_rows(x, g_ref[...]).astype(BF16)

    l0, l1, l2 = l0_ref[0], l1_ref[0], l2_ref[0]
    m = jnp.maximum(jnp.maximum(l0, l1), l2)
    e0, e1, e2 = jnp.exp(l0 - m), jnp.exp(l1 - m), jnp.exp(l2 - m)
    o_dl = (e0 * o0_ref[0] + e1 * o1_ref[0] + e2 * o2_ref[0]) / (e0 + e1 + e2)

    q = q_mm_ref[0]
    km = km_ref[0]
    vm = vm_ref[0]
    heads = range(HEADS_PER_BRANCH)
    cols = [slice(hd * HEAD_DIM, (hd + 1) * HEAD_DIM) for hd in heads]
    s = [_dot_nt(q[:, cols[hd]], km[:, cols[hd]]) for hd in heads]
    p = [jnp.exp(s[hd] - jnp.max(s[hd], axis=-1, keepdims=True)) for hd in heads]
    pv = [_dot(p[hd].astype(BF16), vm[:, cols[hd]]) for hd in heads]
    o_mm = jnp.concatenate([pv[hd] / jnp.sum(p[hd], axis=-1, keepdims=True) for hd in heads], axis=1)

    merged = None
    for br, o in enumerate((o_sb_ref[0], o_fx_ref[0], o_dl, o_mm)):
        gated = (o * z_ref[0, :, br * BRANCH_WIDTH:(br + 1) * BRANCH_WIDTH]).astype(BF16)
        y = _dot(gated, wbr_ref[br])
        gate = _sigmoid(_dot(h, wg_ref[:, br * D_MODEL:(br + 1) * D_MODEL]))
        merged = gate * y if merged is None else merged + gate * y
    y_ref[0] = x + _dot(merged.astype(BF16), wout_ref[...])


def _out_layer(x, ln_g, o_sb, o_fx, dil, q_mm, km, vm, z, w_gate, w_br, w_out, tm):
    b, s, _ = x.shape
    bw = BRANCH_WIDTH
    n_mem = km.shape[1]

    def tok_spec(width):
        return pl.BlockSpec((1, tm, width), lambda i, j: (i, j, 0))

    def const_spec(shape):
        return pl.BlockSpec(shape, lambda i, j: (0,) * len(shape))

    mem_spec = pl.BlockSpec((1, n_mem, bw), lambda i, j: (i, 0, 0))
    dil_args = [t for pair in dil for t in pair]
    return pl.pallas_call(
        _out_kernel,
        grid=(b, s // tm),
        in_specs=([tok_spec(D_MODEL), const_spec((1, D_MODEL)), tok_spec(bw), tok_spec(bw)]
                  + [tok_spec(bw)] * (2 * N_DIL)
                  + [tok_spec(bw), mem_spec, mem_spec, tok_spec(N_BRANCH * bw),
                     const_spec((D_MODEL, N_BRANCH * D_MODEL)), const_spec((N_BRANCH, bw, D_MODEL)),
                     const_spec((D_MODEL, D_MODEL))]),
        out_specs=tok_spec(D_MODEL),
        out_shape=jax.ShapeDtypeStruct((b, s, D_MODEL), F32),
        compiler_params=pltpu.CompilerParams(dimension_semantics=("arbitrary", "arbitrary"),
                                             vmem_limit_bytes=V7X_VMEM_LIMIT_BYTES),
        name="out_layer",
    )(x, ln_g, o_sb, o_fx, *dil_args, q_mm, km, vm, z, w_gate, w_br, w_out)


def _tile(n, pref):
    t = min(pref, n)
    assert n % t == 0, (n, t)
    return t


def _rope_tables(s):
    half = HEAD_DIM // 2
    inv = ROPE_THETA ** (-jnp.arange(half, dtype=F32) / half)
    ang = jnp.arange(s, dtype=F32)[:, None] * inv[None, :]
    cos, sin = jnp.cos(ang), jnp.sin(ang)
    cos_t = jnp.tile(jnp.concatenate([cos, cos], axis=1), (1, HEADS_PER_BRANCH))
    sin_t = jnp.tile(jnp.concatenate([-sin, sin], axis=1), (1, HEADS_PER_BRANCH))
    return cos_t, sin_t


def _fox_placement():
    fox_w = HEADS_PER_BRANCH * FOX_HEAD_LANES
    place = np.zeros((BRANCH_WIDTH, fox_w), np.float32)
    place_f = np.zeros((FOX_BIAS_TERMS, V7X_LANES, fox_w), np.float32)
    q_const = np.zeros((1, fox_w), np.float32)
    k_ones = np.zeros((1, fox_w), np.float32)
    bound_lane = np.zeros((1, fox_w), np.float32)
    for hd in range(HEADS_PER_BRANCH):
        base = hd * FOX_HEAD_LANES
        for d in range(HEAD_DIM):
            place[hd * HEAD_DIM + d, base + d] = 1.0
        for t in range(FOX_BIAS_TERMS):
            place_f[t, hd, base + FOX_LANE_KEY_F + t] = 1.0
            place_f[t, hd, base + FOX_LANE_QRY_F + t] = 1.0
            q_const[0, base + FOX_LANE_KEY_F + t] = -1.0
            k_ones[0, base + FOX_LANE_QRY_F + t] = 1.0
        q_const[0, base + FOX_LANE_BOUND] = -1.0
        bound_lane[0, base + FOX_LANE_BOUND] = 1.0
    return (jnp.asarray(place, BF16), jnp.asarray(place_f, BF16), jnp.asarray(q_const), jnp.asarray(k_ones),
            jnp.asarray(bound_lane))


def kernel(x, mem, ln_gain, mem_ln_gain, qk_gain, w_in, b_forget, w_mem_kv, w_br_sb, w_br_fox, w_br_dil,
           w_br_mem, w_out):
    depth = w_in.shape[0]
    b, s, _ = x.shape
    bw = BRANCH_WIDTH
    tm_proj = _tile(s, 512)
    tq = _tile(s, 256)
    tm_out = _tile(s, 256)
    assert tm_proj % FOX_BLOCK == 0 and s % FOX_BLOCK == 0

    head_id = np.arange(bw) // HEAD_DIM
    group_mean = jnp.asarray((head_id[:, None] == head_id[None, :]) / HEAD_DIM, BF16)
    tri_prefix = jnp.asarray(np.arange(tm_proj)[:, None] >= np.arange(tm_proj)[None, :], BF16)
    tri_later = jnp.asarray(np.arange(tq)[:, None] > np.arange(tq)[None, :], BF16)
    cos_t, sin_t = _rope_tables(s)
    place, place_f, q_const, k_ones, bound_lane = _fox_placement()

    gains = jnp.tile(qk_gain, (1, 1, HEADS_PER_BRANCH))
    km_all, vm_all = _mem_kv(mem, mem_ln_gain[:, None, :], w_mem_kv.astype(BF16), gains[:, 5:6, :], group_mean)

    assert w_in.shape[2] == (_N_SEC + N_BRANCH * D_MODEL // bw) * bw + HEADS_PER_BRANCH, w_in.shape
    main_chunks = _N_SEC * bw // PREP_CHUNK
    w_main_all = _prep_weights(w_in, 0, main_chunks)
    w_gate_all = _prep_weights(w_in, main_chunks, N_BRANCH * D_MODEL // PREP_CHUNK)
    wf_all = jnp.pad(w_in[:, :, _FORGET_COL:_FORGET_COL + HEADS_PER_BRANCH],
                     ((0, 0), (0, 0), (0, V7X_LANES - HEADS_PER_BRANCH))).astype(BF16)

    for l in range(depth):
        w_main, w_gate, wf = w_main_all[l], w_gate_all[l], wf_all[l]
        bf = jnp.pad(b_forget[l], (0, V7X_LANES - HEADS_PER_BRANCH))[None, :]
        gains_l = jnp.pad(gains[l], ((0, 2), (0, 0)))
        ln_g = ln_gain[l][None, :]

        qk_bound = (FOX_BOUND_MARGIN * HEAD_DIM ** 0.5
                    * jnp.max(jnp.abs(qk_gain[l, 0])) * jnp.max(jnp.abs(qk_gain[l, 1])))
        k_const = k_ones + qk_bound * bound_lane

        outs = _proj(x, ln_g, w_main, wf, bf, gains_l, cos_t, sin_t, group_mean, tri_prefix, place, place_f,
                     q_const, k_const, tm_proj)
        q_sb, k_sb, v_sb, q_fx, k_fx, vt_fx = outs[:6]
        dil_qkv = outs[6:6 + 3 * N_DIL]
        q_mm, z = outs[6 + 3 * N_DIL:]

        o_sb = _sb_attention(q_sb, k_sb, v_sb, tri_later, tq)
        o_fx = lax.cond(qk_bound <= FOX_SAFE_BOUND,
                        functools.partial(_fox_attention, bounded=True),
                        functools.partial(_fox_attention, bounded=False), q_fx, k_fx, vt_fx)
        dil = [_dil_attention(*dil_qkv[3 * g:3 * g + 3], window, dilation)
               for g, (window, dilation) in enumerate(DIL_PATTERNS)]
        w_br = jnp.stack([w_br_sb[l], w_br_fox[l], w_br_dil[l], w_br_mem[l]]).astype(BF16)
        x = _out_layer(x, ln_g, o_sb, o_fx, dil, q_mm, km_all[l], vm_all[l], z, w_gate, w_br,
                       w_out[l].astype(BF16), tm_out)
    return x
```

```python
import functools

import jax
import jax.numpy as jnp
import numpy as np
from jax import lax
from jax.experimental import pallas as pl
from jax.experimental.pallas import tpu as pltpu

F32 = jnp.float32
BF16 = jnp.bfloat16

D_MODEL = 1024
HEAD_DIM = 64
HEADS_PER_BRANCH = 4
BRANCH_WIDTH = HEADS_PER_BRANCH * HEAD_DIM
DIL_PATTERNS = ((128, 1), (512, 4), (2048, 16))
N_DIL = len(DIL_PATTERNS)
N_BRANCH = 4
DIL_BLOCK = 128
DIL_SPAN = 2048
ROPE_THETA = 10000.0
EPS = 1e-6
MASKED_LOGIT = -1e4
QK_SCALE = HEAD_DIM ** -0.5
SB_UNDERFLOW = -105.0
V7X_VMEM_LIMIT_BYTES = 56 * 1024 * 1024
V7X_LANES = 128
V7X_BF16_SUBLANES = 16
FOX_HEAD_LANES = V7X_LANES
FOX_BIAS_TERMS = 3
FOX_LANE_KEY_F = HEAD_DIM
FOX_LANE_QRY_F = HEAD_DIM + FOX_BIAS_TERMS
FOX_LANE_BOUND = HEAD_DIM + 2 * FOX_BIAS_TERMS
FOX_BOUND_MARGIN = 1.02
FOX_SAFE_BOUND = 30.0
FOX_VT_ROWS = HEAD_DIM + V7X_BF16_SUBLANES
FOX_BLOCK = 512

_NT = (((1,), (1,)), ((), ()))


def _dot(a, b):
    return jnp.dot(a, b, preferred_element_type=F32)


def _dot_nt(a, b):
    return lax.dot_general(a, b, _NT, preferred_element_type=F32)


def _bf16_terms(x, terms):
    out = []
    r = x
    for t in range(terms):
        p = r.astype(BF16)
        out.append(p)
        if t + 1 < terms:
            r = r - p.astype(F32)
    return out


def _rms_rows(x, gain):
    ms = jnp.mean(x * x, axis=-1, keepdims=True)
    return x * lax.rsqrt(ms + EPS) * gain


def _head_rms(x, group_mean, gain):
    ms = sum(_dot(p, group_mean) for p in _bf16_terms(x * x, 2))
    return x * lax.rsqrt(ms + EPS) * gain


def _sigmoid(x):
    return 1.0 / (1.0 + jnp.exp(-x))


def _log_sigmoid(x):
    return jnp.minimum(x, 0.0) - jnp.log(1.0 + jnp.exp(-jnp.abs(x)))


def _rope(x, cos, sin_signed):
    lane = lax.broadcasted_iota(jnp.int32, x.shape, 1)
    first_half = (lane & (HEAD_DIM - 1)) < (HEAD_DIM // 2)
    w = x.shape[1]
    partner = jnp.where(first_half, pltpu.roll(x, w - HEAD_DIM // 2, 1), pltpu.roll(x, HEAD_DIM // 2, 1))
    return x * cos + partner * sin_signed


def _mem_kv_kernel(mem_ref, g_ref, w_ref, gk_ref, gm_ref, km_ref, vm_ref):
    h = _rms_rows(mem_ref[0], g_ref[0]).astype(BF16)
    kv = _dot(h, w_ref[0])
    km_ref[0, 0] = _head_rms(kv[:, :BRANCH_WIDTH], gm_ref[...], gk_ref[0]).astype(BF16)
    vm_ref[0, 0] = kv[:, BRANCH_WIDTH:].astype(BF16)


def _mem_kv(mem, mem_ln_gain, w_mem_kv, gain_k, group_mean):
    depth = w_mem_kv.shape[0]
    b, n_mem, _ = mem.shape
    out = jax.ShapeDtypeStruct((depth, b, n_mem, BRANCH_WIDTH), BF16)
    return pl.pallas_call(
        _mem_kv_kernel,
        grid=(depth, b),
        in_specs=[
            pl.BlockSpec((1, n_mem, D_MODEL), lambda l, i: (i, 0, 0)),
            pl.BlockSpec((1, 1, D_MODEL), lambda l, i: (l, 0, 0)),
            pl.BlockSpec((1, D_MODEL, 2 * BRANCH_WIDTH), lambda l, i: (l, 0, 0)),
            pl.BlockSpec((1, 1, BRANCH_WIDTH), lambda l, i: (l, 0, 0)),
            pl.BlockSpec((BRANCH_WIDTH, BRANCH_WIDTH), lambda l, i: (0, 0)),
        ],
        out_specs=[pl.BlockSpec((1, 1, n_mem, BRANCH_WIDTH), lambda l, i: (l, i, 0, 0))] * 2,
        out_shape=[out, out],
        compiler_params=pltpu.CompilerParams(dimension_semantics=("arbitrary", "arbitrary")),
        name="mem_kv",
    )(mem, mem_ln_gain, w_mem_kv, gain_k, group_mean)


_SEC_Q_SB, _SEC_K_SB, _SEC_V_SB, _SEC_Z_SB = 0, 1, 2, 3
_SEC_Q_FX, _SEC_K_FX, _SEC_V_FX, _SEC_Z_FX = 4, 5, 6, 7
_SEC_Q_DL, _SEC_K_DL, _SEC_V_DL, _SEC_Z_DL = 8, 11, 14, 17
_SEC_Q_MM, _SEC_Z_MM = 18, 19
_SEC_Z = (_SEC_Z_SB, _SEC_Z_FX, _SEC_Z_DL, _SEC_Z_MM)
_N_SEC = 20
_FORGET_COL = 8 * BRANCH_WIDTH
PREP_CHUNK = 512


def _prep_kernel(a_ref, b_ref, o_ref, *, first_chunk):
    shifted = jnp.concatenate([a_ref[0], b_ref[0]], axis=1)[:, HEADS_PER_BRANCH:HEADS_PER_BRANCH + PREP_CHUNK]
    if first_chunk * PREP_CHUNK >= _FORGET_COL:
        o_ref[0] = shifted.astype(BF16)
    else:
        before_forget = (pl.program_id(1) + first_chunk) * PREP_CHUNK < _FORGET_COL
        o_ref[0] = jnp.where(before_forget, a_ref[0], shifted).astype(BF16)


def _prep_weights(w_in, first_chunk, n_chunks):
    depth, d_model, _ = w_in.shape
    tail = PREP_CHUNK // V7X_LANES
    return pl.pallas_call(
        functools.partial(_prep_kernel, first_chunk=first_chunk),
        grid=(depth, n_chunks),
        in_specs=[pl.BlockSpec((1, d_model, PREP_CHUNK), lambda l, c: (l, 0, c + first_chunk)),
                  pl.BlockSpec((1, d_model, V7X_LANES), lambda l, c: (l, 0, tail * (c + first_chunk + 1)))],
        out_specs=pl.BlockSpec((1, d_model, PREP_CHUNK), lambda l, c: (l, 0, c)),
        out_shape=jax.ShapeDtypeStruct((depth, d_model, n_chunks * PREP_CHUNK), BF16),
        compiler_params=pltpu.CompilerParams(dimension_semantics=("arbitrary", "arbitrary")),
        name="prep_weights",
    )(w_in, w_in)


def _proj_kernel(x_ref, g_ref, w_ref, wf_ref, bf_ref, gains_ref, cos_ref, sin_ref, gm_ref, tri_ref,
                 place_ref, place_f_ref, q_const_ref, k_const_ref,
                 q_sb_ref, k_sb_ref, v_sb_ref, q_fx_ref, k_fx_ref, vt_fx_ref, *rest):
    dil_refs = rest[:3 * N_DIL]
    q_mm_ref, z_ref, stride_ref, carry_ref = rest[3 * N_DIL:]
    tm = x_ref.shape[1]
    h = _rms_rows(x_ref[0], g_ref[...]).astype(BF16)
    gm = gm_ref[...]

    def sec(i):
        return _dot(h, w_ref[:, i * BRANCH_WIDTH:(i + 1) * BRANCH_WIDTH])

    def gain(i):
        return gains_ref[i:i + 1, :]

    @pl.when(pl.program_id(1) == 0)
    def _():
        carry_ref[...] = jnp.zeros_like(carry_ref)

    log_f = _log_sigmoid(_dot(h, wf_ref[...]) + bf_ref[...])

    q_sb_ref[0] = (sec(_SEC_Q_SB) * QK_SCALE).astype(BF16)
    k_sb_ref[0] = sec(_SEC_K_SB).astype(BF16)
    v_sb_ref[0] = sec(_SEC_V_SB).astype(BF16)

    fcum = sum(_dot(tri_ref[...], p) for p in _bf16_terms(log_f, 3)) + carry_ref[...]
    carry_ref[...] = fcum[tm - 1:tm, :]

    for c in range(N_BRANCH):
        z = sec(_SEC_Z[c])
        z_ref[0, :, c * BRANCH_WIDTH:(c + 1) * BRANCH_WIDTH] = z * _sigmoid(z)

    place = place_ref[...]
    q_fx = (_head_rms(sec(_SEC_Q_FX), gm, gain(0)) * QK_SCALE).astype(BF16)
    k_fx = _head_rms(sec(_SEC_K_FX), gm, gain(1)).astype(BF16)
    f_lanes = sum(_dot(p, place_f_ref[t]) for t, p in enumerate(_bf16_terms(fcum, FOX_BIAS_TERMS)))
    lane = lax.broadcasted_iota(jnp.int32, f_lanes.shape, 1) & (FOX_HEAD_LANES - 1)
    key_f = (lane >= FOX_LANE_KEY_F) & (lane < FOX_LANE_QRY_F)
    q_fx_ref[0] = (_dot(q_fx, place) + jnp.where(key_f, 0.0, f_lanes) + q_const_ref[...]).astype(BF16)
    k_fx_ref[0] = (_dot(k_fx, place) + jnp.where(key_f, f_lanes, 0.0) + k_const_ref[...]).astype(BF16)
    vt = sec(_SEC_V_FX).T.astype(BF16)
    ones_rows = (lax.broadcasted_iota(jnp.int32, (V7X_BF16_SUBLANES, FOX_BLOCK), 0) == 0).astype(BF16)
    for blk in range(tm // FOX_BLOCK):
        for hd in range(HEADS_PER_BRANCH):
            r0 = hd * FOX_VT_ROWS
            vt_fx_ref[0, blk, r0:r0 + HEAD_DIM, :] = vt[hd * HEAD_DIM:(hd + 1) * HEAD_DIM,
                                                        blk * FOX_BLOCK:(blk + 1) * FOX_BLOCK]
            vt_fx_ref[0, blk, r0 + HEAD_DIM:r0 + FOX_VT_ROWS, :] = ones_rows

    cos = cos_ref[...]
    sin = sin_ref[...]
    slot = 0
    for c, (_, dilation) in enumerate(DIL_PATTERNS):
        q = _rope(_head_rms(sec(_SEC_Q_DL + c), gm, gain(2)), cos, sin) * QK_SCALE
        k = _rope(_head_rms(sec(_SEC_K_DL + c), gm, gain(3)), cos, sin)
        v = sec(_SEC_V_DL + c)
        for t, val in enumerate((q, k, v)):
            out_ref = dil_refs[3 * c + t]
            if dilation == 1:
                out_ref[0, 0] = val.astype(BF16)
            else:
                for half in range(BRANCH_WIDTH // V7X_LANES):
                    stride_ref[slot, half] = val[:, half * V7X_LANES:(half + 1) * V7X_LANES]
                for r in range(dilation):
                    rows = pl.ds(r, tm // dilation, stride=dilation)
                    out_ref[0, r] = jnp.concatenate([stride_ref[slot, half, rows, :]
                                                     for half in range(BRANCH_WIDTH // V7X_LANES)], axis=1).astype(BF16)
                slot += 1

    q_mm_ref[0] = (_head_rms(sec(_SEC_Q_MM), gm, gain(4)) * QK_SCALE).astype(BF16)


def _proj(x, ln_g, w_main, wf, bf, gains, cos_t, sin_t, group_mean, tri, place, place_f, q_const, k_const, tm):
    b, s, _ = x.shape
    bw = BRANCH_WIDTH
    fox_w = HEADS_PER_BRANCH * FOX_HEAD_LANES
    vt_rows = HEADS_PER_BRANCH * FOX_VT_ROWS

    def tok(width, dtype):
        return jax.ShapeDtypeStruct((b, s, width), dtype)

    def tok_spec(width):
        return pl.BlockSpec((1, tm, width), lambda i, j: (i, j, 0))

    def const_spec(shape):
        return pl.BlockSpec(shape, lambda i, j: (0,) * len(shape))

    dil_shapes, dil_specs = [], []
    for _, d in DIL_PATTERNS:
        dil_shapes += [jax.ShapeDtypeStruct((b, d, s // d, bw), BF16)] * 3
        dil_specs += [pl.BlockSpec((1, d, tm // d, bw), lambda i, j: (i, 0, j, 0))] * 3

    out_shape = ([tok(bw, BF16)] * 3 + [tok(fox_w, BF16)] * 2
                 + [jax.ShapeDtypeStruct((b, s // FOX_BLOCK, vt_rows, FOX_BLOCK), BF16)]
                 + dil_shapes + [tok(bw, BF16), tok(N_BRANCH * bw, F32)])
    out_specs = ([tok_spec(bw)] * 3 + [tok_spec(fox_w)] * 2
                 + [pl.BlockSpec((1, tm // FOX_BLOCK, vt_rows, FOX_BLOCK), lambda i, j: (i, j, 0, 0))]
                 + dil_specs + [tok_spec(bw), tok_spec(N_BRANCH * bw)])
    return pl.pallas_call(
        _proj_kernel,
        grid=(b, s // tm),
        in_specs=[
            tok_spec(D_MODEL),
            const_spec((1, D_MODEL)),
            const_spec((D_MODEL, _N_SEC * bw)),
            const_spec((D_MODEL, V7X_LANES)),
            const_spec((1, V7X_LANES)),
            const_spec((8, bw)),
            pl.BlockSpec((tm, bw), lambda i, j: (j, 0)),
            pl.BlockSpec((tm, bw), lambda i, j: (j, 0)),
            const_spec((bw, bw)),
            const_spec((tm, tm)),
            const_spec((bw, fox_w)),
            const_spec((FOX_BIAS_TERMS, V7X_LANES, fox_w)),
            const_spec((1, fox_w)),
            const_spec((1, fox_w)),
        ],
        out_specs=out_specs,
        out_shape=out_shape,
        scratch_shapes=[pltpu.VMEM((3 * sum(d > 1 for _, d in DIL_PATTERNS), bw // V7X_LANES, tm, V7X_LANES), F32),
                        pltpu.VMEM((1, V7X_LANES), F32)],
        compiler_params=pltpu.CompilerParams(dimension_semantics=("arbitrary", "arbitrary"),
                                             vmem_limit_bytes=V7X_VMEM_LIMIT_BYTES),
        name="proj",
    )(x, ln_g, w_main, wf, bf, gains, cos_t, sin_t, group_mean, tri, place, place_f, q_const, k_const)


def _sb_kernel(q_ref, k_ref, v_ref, tri_ref, o_ref, acc_ref, c_ref, *, tq):
    i = pl.program_id(1)
    q = q_ref[0]
    tri = tri_ref[...]
    acc_ref[...] = jnp.zeros_like(acc_ref)
    c_ref[...] = jnp.zeros_like(c_ref)
    heads = range(HEADS_PER_BRANCH)
    cols = [slice(hd * HEAD_DIM, (hd + 1) * HEAD_DIM) for hd in heads]

    def block(j, diagonal):
        k0 = pl.multiple_of(j * tq, tq)
        kb = k_ref[0, pl.ds(k0, tq), :]
        vb = v_ref[0, pl.ds(k0, tq), :]
        z = [_dot_nt(q[:, cols[hd]], kb[:, cols[hd]]) for hd in heads]
        if diagonal:
            row = lax.broadcasted_iota(jnp.int32, (tq, tq), 0)
            col = lax.broadcasted_iota(jnp.int32, (tq, tq), 1)
            z = [jnp.where(col < row, z[hd], MASKED_LOGIT) for hd in heads]
        t = [jnp.log(1.0 + jnp.exp(-jnp.abs(z[hd]))) for hd in heads]
        log_beta = [jnp.minimum(z[hd], 0.0) - t[hd] for hd in heads]
        log_keep = [log_beta[hd] - z[hd] for hd in heads]
        c = [c_ref[hd] for hd in heads]
        later = [_dot(log_keep[hd].astype(BF16), tri) + c[hd] for hd in heads]
        a = [jnp.exp(log_beta[hd] + later[hd]).astype(BF16) for hd in heads]
        pv = [_dot(a[hd], vb[:, cols[hd]]) for hd in heads]
        c_max = jnp.float32(-jnp.inf)
        for hd in heads:
            acc_ref[:, cols[hd]] += pv[hd]
            c_new = c[hd] + jnp.sum(log_keep[hd], axis=-1, keepdims=True)
            c_ref[hd] = c_new
            c_max = jnp.maximum(c_max, jnp.max(c_new))
        return c_max

    def cond(state):
        j, c_max = state
        return jnp.logical_and(j >= 0, c_max > SB_UNDERFLOW)

    def body(state):
        j, _ = state
        return j - 1, block(j, False)

    lax.while_loop(cond, body, (i - 1, block(i, True)))
    o_ref[0] = acc_ref[...]


def _sb_attention(q, k, v, tri_strict, tq):
    b, s, w = q.shape
    return pl.pallas_call(
        functools.partial(_sb_kernel, tq=tq),
        grid=(b, s // tq),
        in_specs=[
            pl.BlockSpec((1, tq, w), lambda i, j: (i, j, 0)),
            pl.BlockSpec((1, s, w), lambda i, j: (i, 0, 0)),
            pl.BlockSpec((1, s, w), lambda i, j: (i, 0, 0)),
            pl.BlockSpec((tq, tq), lambda i, j: (0, 0)),
        ],
        out_specs=pl.BlockSpec((1, tq, w), lambda i, j: (i, j, 0)),
        out_shape=jax.ShapeDtypeStruct((b, s, w), F32),
        scratch_shapes=[pltpu.VMEM((tq, w), F32), pltpu.VMEM((HEADS_PER_BRANCH, tq, 1), F32)],
        compiler_params=pltpu.CompilerParams(dimension_semantics=("arbitrary", "arbitrary"),
                                             vmem_limit_bytes=V7X_VMEM_LIMIT_BYTES),
        name="sb_attention",
    )(q, k, v, tri_strict)


def _fox_scores(k_ref, q, j, masked):
    k0 = pl.multiple_of(j * FOX_BLOCK, FOX_BLOCK)
    kb = k_ref[0, pl.ds(k0, FOX_BLOCK), :]
    scores = []
    for hd in range(HEADS_PER_BRANCH):
        lanes = slice(hd * FOX_HEAD_LANES, (hd + 1) * FOX_HEAD_LANES)
        s = _dot_nt(kb[:, lanes], q[:, lanes])
        if masked:
            key = lax.broadcasted_iota(jnp.int32, s.shape, 0)
            qry = lax.broadcasted_iota(jnp.int32, s.shape, 1)
            s = jnp.where(key <= qry, s, -jnp.inf)
        scores.append(s)
    return scores


def _fox_values(vt_ref, j, hd):
    return vt_ref[0, j, hd * FOX_VT_ROWS:(hd + 1) * FOX_VT_ROWS, :]


def _fox_finish(acc_ref, o_ref):
    for hd in range(HEADS_PER_BRANCH):
        acc = acc_ref[hd]
        o_t = acc[:HEAD_DIM] / acc[HEAD_DIM:HEAD_DIM + 1]
        o_ref[0, :, hd * HEAD_DIM:(hd + 1) * HEAD_DIM] = o_t.T


def _fox_bounded_kernel(q_ref, k_ref, vt_ref, o_ref, acc_ref):
    i = pl.program_id(1)
    q = q_ref[0]
    heads = range(HEADS_PER_BRANCH)
    acc_ref[...] = jnp.zeros_like(acc_ref)

    def step(blocks, masked):
        scores = [_fox_scores(k_ref, q, j, masked) for j in blocks]
        probs = [[jnp.exp(s[hd]).astype(BF16) for hd in heads] for s in scores]
        pv = [[_dot(_fox_values(vt_ref, j, hd), p[hd]) for hd in heads] for j, p in zip(blocks, probs)]
        for hd in heads:
            acc_ref[hd] += sum(block_pv[hd] for block_pv in pv)

    def two_blocks(pair, carry):
        step((2 * pair, 2 * pair + 1), False)
        return carry

    lax.fori_loop(0, i // 2, two_blocks, 0)

    @pl.when(i % 2 == 1)
    def _():
        step((i - 1,), False)

    step((i,), True)
    _fox_finish(acc_ref, o_ref)


def _fox_online_kernel(q_ref, k_ref, vt_ref, o_ref, acc_ref, m_ref):
    i = pl.program_id(1)
    q = q_ref[0]
    heads = range(HEADS_PER_BRANCH)
    acc_ref[...] = jnp.zeros_like(acc_ref)
    m_ref[...] = jnp.full_like(m_ref, -jnp.inf)

    def step(j, masked):
        scores = _fox_scores(k_ref, q, j, masked)
        m_old = [m_ref[hd] for hd in heads]
        m_new = [jnp.maximum(m_old[hd], jnp.max(scores[hd], axis=0, keepdims=True)) for hd in heads]
        probs = [jnp.exp(scores[hd] - m_new[hd]).astype(BF16) for hd in heads]
        pv = [_dot(_fox_values(vt_ref, j, hd), probs[hd]) for hd in heads]
        for hd in heads:
            acc_ref[hd] = jnp.exp(m_old[hd] - m_new[hd]) * acc_ref[hd] + pv[hd]
            m_ref[hd] = m_new[hd]

    def body(j, carry):
        step(j, False)
        return carry

    lax.fori_loop(0, i, body, 0)
    step(i, True)
    _fox_finish(acc_ref, o_ref)


def _fox_attention(q, k, vt, bounded):
    b, s, w = q.shape
    tq = FOX_BLOCK
    n_blk, vt_rows, _ = vt.shape[1:]
    scratch = [pltpu.VMEM((HEADS_PER_BRANCH, FOX_VT_ROWS, tq), F32)]
    if not bounded:
        scratch.append(pltpu.VMEM((HEADS_PER_BRANCH, 1, tq), F32))
    return pl.pallas_call(
        _fox_bounded_kernel if bounded else _fox_online_kernel,
        grid=(b, s // tq),
        in_specs=[
            pl.BlockSpec((1, tq, w), lambda i, j: (i, j, 0)),
            pl.BlockSpec((1, s, w), lambda i, j: (i, 0, 0)),
            pl.BlockSpec((1, n_blk, vt_rows, FOX_BLOCK), lambda i, j: (i, 0, 0, 0)),
        ],
        out_specs=pl.BlockSpec((1, tq, BRANCH_WIDTH), lambda i, j: (i, j, 0)),
        out_shape=jax.ShapeDtypeStruct((b, s, BRANCH_WIDTH), F32),
        scratch_shapes=scratch,
        compiler_params=pltpu.CompilerParams(dimension_semantics=("arbitrary", "arbitrary"),
                                             vmem_limit_bytes=V7X_VMEM_LIMIT_BYTES),
        name="fox_attention_bounded" if bounded else "fox_attention_online",
    )(q, k, vt)


def _dil_kernel(q_ref, kc_ref, kp_ref, vc_ref, vp_ref, o_ref, lse_ref, stride_ref, *, n_back, dilation):
    span_u = q_ref.shape[2]
    n_blk = span_u // DIL_BLOCK
    heads = range(HEADS_PER_BRANCH)
    heads_per_tile = V7X_LANES // HEAD_DIM
    n_tiles = BRANCH_WIDTH // V7X_LANES
    key = lax.broadcasted_iota(jnp.int32, (2 * DIL_BLOCK, DIL_BLOCK), 0)
    qry = lax.broadcasted_iota(jnp.int32, (2 * DIL_BLOCK, DIL_BLOCK), 1)
    dist = qry + DIL_BLOCK - key
    in_window = (dist >= 0) & (dist <= n_back)
    first_key = jnp.where(pl.program_id(1) == 0, DIL_BLOCK, 0)
    in_window_first = in_window & (key >= first_key)
    lane = lax.broadcasted_iota(jnp.int32, (DIL_BLOCK, V7X_LANES), 1)
    head_lanes = [(lane >= i * HEAD_DIM) & (lane < (i + 1) * HEAD_DIM) for i in range(heads_per_tile)]

    def transposed(v):
        return v.astype(F32).T.astype(BF16)

    for r in range(dilation):
        vt_prev = transposed(vp_ref[0, r, span_u - DIL_BLOCK:, :])
        for ub in range(n_blk):
            rows = slice(ub * DIL_BLOCK, (ub + 1) * DIL_BLOCK)
            q = q_ref[0, r, rows, :]
            k_prev = (kp_ref[0, r, span_u - DIL_BLOCK:, :] if ub == 0
                      else kc_ref[0, r, (ub - 1) * DIL_BLOCK:ub * DIL_BLOCK, :])
            kcat = jnp.concatenate([k_prev, kc_ref[0, r, rows, :]], axis=0)
            vt_own = transposed(vc_ref[0, r, rows, :])
            valid = in_window_first if ub == 0 else in_window
            s = []
            for hd in heads:
                tile = slice(hd // heads_per_tile * V7X_LANES, (hd // heads_per_tile + 1) * V7X_LANES)
                q_hd = jnp.where(head_lanes[hd % heads_per_tile], q[:, tile], jnp.zeros_like(q[:, tile]))
                s.append(jnp.where(valid, _dot_nt(kcat[:, tile], q_hd), -jnp.inf))
            m = [jnp.max(s[hd], axis=0, keepdims=True) for hd in heads]
            p = [jnp.exp(s[hd] - m[hd]) for hd in heads]
            denom = [jnp.sum(p[hd], axis=0, keepdims=True) for hd in heads]
            pb = [p[hd].astype(BF16) for hd in heads]
            dims = [slice(hd * HEAD_DIM, (hd + 1) * HEAD_DIM) for hd in heads]
            pv = [_dot(vt_prev[dims[hd], :], pb[hd][:DIL_BLOCK]) + _dot(vt_own[dims[hd], :], pb[hd][DIL_BLOCK:])
                  for hd in heads]
            outs = [pv[hd] / denom[hd] for hd in heads]
            lses = [jnp.broadcast_to(m[hd] + jnp.log(denom[hd]), (HEAD_DIM, DIL_BLOCK)) for hd in heads]
            vt_prev = vt_own
            for t, (vals, out_ref) in enumerate(((outs, o_ref), (lses, lse_ref))):
                for tile in range(n_tiles):
                    pair = vals[tile * heads_per_tile:(tile + 1) * heads_per_tile]
                    val = jnp.concatenate(pair, axis=0).T
                    if dilation == 1:
                        out_ref[0, rows, tile * V7X_LANES:(tile + 1) * V7X_LANES] = val
                    else:
                        dst = pl.ds(ub * DIL_BLOCK * dilation + r, DIL_BLOCK, stride=dilation)
                        stride_ref[t, tile, dst, :] = val
    if dilation > 1:
        for t, out_ref in enumerate((o_ref, lse_ref)):
            for tile in range(n_tiles):
                out_ref[0, :, tile * V7X_LANES:(tile + 1) * V7X_LANES] = stride_ref[t, tile]


def _dil_attention(q, k, v, window, dilation):
    b, _, l, w = q.shape
    s = l * dilation
    span = min(DIL_SPAN, s)
    assert s % span == 0 and (span // dilation) % DIL_BLOCK == 0, (s, span, dilation)
    blk = (1, dilation, span // dilation, w)

    def cur(i, j):
        return (i, 0, j, 0)

    def prev(i, j):
        return (i, 0, jnp.maximum(j - 1, 0), 0)

    out = jax.ShapeDtypeStruct((b, s, w), F32)
    return pl.pallas_call(
        functools.partial(_dil_kernel, n_back=window // dilation, dilation=dilation),
        grid=(b, s // span),
        in_specs=[pl.BlockSpec(blk, cur), pl.BlockSpec(blk, cur), pl.BlockSpec(blk, prev),
                  pl.BlockSpec(blk, cur), pl.BlockSpec(blk, prev)],
        out_specs=[pl.BlockSpec((1, span, w), lambda i, j: (i, j, 0))] * 2,
        out_shape=[out, out],
        scratch_shapes=[pltpu.VMEM((2, w // V7X_LANES, span, V7X_LANES), F32)],
        compiler_params=pltpu.CompilerParams(dimension_semantics=("arbitrary", "arbitrary"),
                                             vmem_limit_bytes=V7X_VMEM_LIMIT_BYTES),
        name=f"dil_attention_d{dilation}",
    )(q, k, k, v, v)


def _out_kernel(x_ref, g_ref, o_sb_ref, o_fx_ref, o0_ref, l0_ref, o1_ref, l1_ref, o2_ref, l2_ref,
                q_mm_ref, km_ref, vm_ref, z_ref, wg_ref, wbr_ref, wout_ref, y_ref):
    x = x_ref[0]
    h = _rms_rows(x, g_ref[...]).astype(BF16)

    l0, l1, l2 = l0_ref[0], l1_ref[0], l2_ref[0]
    m = jnp.maximum(jnp.maximum(l0, l1), l2)
    e0, e1, e2 = jnp.exp(l0 - m), jnp.exp(l1 - m), jnp.exp(l2 - m)
    o_dl = (e0 * o0_ref[0] + e1 * o1_ref[0] + e2 * o2_ref[0]) / (e0 + e1 + e2)

    q = q_mm_ref[0]
    km = km_ref[0]
    vm = vm_ref[0]
    heads = range(HEADS_PER_BRANCH)
    cols = [slice(hd * HEAD_DIM, (hd + 1) * HEAD_DIM) for hd in heads]
    s = [_dot_nt(q[:, cols[hd]], km[:, cols[hd]]) for hd in heads]
    p = [jnp.exp(s[hd] - jnp.max(s[hd], axis=-1, keepdims=True)) for hd in heads]
    pv = [_dot(p[hd].astype(BF16), vm[:, cols[hd]]) for hd in heads]
    o_mm = jnp.concatenate([pv[hd] / jnp.sum(p[hd], axis=-1, keepdims=True) for hd in heads], axis=1)

    merged = None
    for br, o in enumerate((o_sb_ref[0], o_fx_ref[0], o_dl, o_mm)):
        gated = (o * z_ref[0, :, br * BRANCH_WIDTH:(br + 1) * BRANCH_WIDTH]).astype(BF16)
        y = _dot(gated, wbr_ref[br])
        gate = _sigmoid(_dot(h, wg_ref[:, br * D_MODEL:(br + 1) * D_MODEL]))
        merged = gate * y if merged is None else merged + gate * y
    y_ref[0] = x + _dot(merged.astype(BF16), wout_ref[...])


def _out_layer(x, ln_g, o_sb, o_fx, dil, q_mm, km, vm, z, w_gate, w_br, w_out, tm):
    b, s, _ = x.shape
    bw = BRANCH_WIDTH
    n_mem = km.shape[1]

    def tok_spec(width):
        return pl.BlockSpec((1, tm, width), lambda i, j: (i, j, 0))

    def const_spec(shape):
        return pl.BlockSpec(shape, lambda i, j: (0,) * len(shape))

    mem_spec = pl.BlockSpec((1, n_mem, bw), lambda i, j: (i, 0, 0))
    dil_args = [t for pair in dil for t in pair]
    return pl.pallas_call(
        _out_kernel,
        grid=(b, s // tm),
        in_specs=([tok_spec(D_MODEL), const_spec((1, D_MODEL)), tok_spec(bw), tok_spec(bw)]
                  + [tok_spec(bw)] * (2 * N_DIL)
                  + [tok_spec(bw), mem_spec, mem_spec, tok_spec(N_BRANCH * bw),
                     const_spec((D_MODEL, N_BRANCH * D_MODEL)), const_spec((N_BRANCH, bw, D_MODEL)),
                     const_spec((D_MODEL, D_MODEL))]),
        out_specs=tok_spec(D_MODEL),
        out_shape=jax.ShapeDtypeStruct((b, s, D_MODEL), F32),
        compiler_params=pltpu.CompilerParams(dimension_semantics=("arbitrary", "arbitrary"),
                                             vmem_limit_bytes=V7X_VMEM_LIMIT_BYTES),
        name="out_layer",
    )(x, ln_g, o_sb, o_fx, *dil_args, q_mm, km, vm, z, w_gate, w_br, w_out)


def _tile(n, pref):
    t = min(pref, n)
    assert n % t == 0, (n, t)
    return t


def _rope_tables(s):
    half = HEAD_DIM // 2
    inv = ROPE_THETA ** (-jnp.arange(half, dtype=F32) / half)
    ang = jnp.arange(s, dtype=F32)[:, None] * inv[None, :]
    cos, sin = jnp.cos(ang), jnp.sin(ang)
    cos_t = jnp.tile(jnp.concatenate([cos, cos], axis=1), (1, HEADS_PER_BRANCH))
    sin_t = jnp.tile(jnp.concatenate([-sin, sin], axis=1), (1, HEADS_PER_BRANCH))
    return cos_t, sin_t


def _fox_placement():
    fox_w = HEADS_PER_BRANCH * FOX_HEAD_LANES
    place = np.zeros((BRANCH_WIDTH, fox_w), np.float32)
    place_f = np.zeros((FOX_BIAS_TERMS, V7X_LANES, fox_w), np.float32)
    q_const = np.zeros((1, fox_w), np.float32)
    k_ones = np.zeros((1, fox_w), np.float32)
    bound_lane = np.zeros((1, fox_w), np.float32)
    for hd in range(HEADS_PER_BRANCH):
        base = hd * FOX_HEAD_LANES
        for d in range(HEAD_DIM):
            place[hd * HEAD_DIM + d, base + d] = 1.0
        for t in range(FOX_BIAS_TERMS):
            place_f[t, hd, base + FOX_LANE_KEY_F + t] = 1.0
            place_f[t, hd, base + FOX_LANE_QRY_F + t] = 1.0
            q_const[0, base + FOX_LANE_KEY_F + t] = -1.0
            k_ones[0, base + FOX_LANE_QRY_F + t] = 1.0
        q_const[0, base + FOX_LANE_BOUND] = -1.0
        bound_lane[0, base + FOX_LANE_BOUND] = 1.0
    return (jnp.asarray(place, BF16), jnp.asarray(place_f, BF16), jnp.asarray(q_const), jnp.asarray(k_ones),
            jnp.asarray(bound_lane))


def kernel(x, mem, ln_gain, mem_ln_gain, qk_gain, w_in, b_forget, w_mem_kv, w_br_sb, w_br_fox, w_br_dil,
           w_br_mem, w_out):
    depth = w_in.shape[0]
    b, s, _ = x.shape
    bw = BRANCH_WIDTH
    tm_proj = _tile(s, 512)
    tq = _tile(s, 256)
    tm_out = _tile(s, 256)
    assert tm_proj % FOX_BLOCK == 0 and s % FOX_BLOCK == 0

    head_id = np.arange(bw) // HEAD_DIM
    group_mean = jnp.asarray((head_id[:, None] == head_id[None, :]) / HEAD_DIM, BF16)
    tri_prefix = jnp.asarray(np.arange(tm_proj)[:, None] >= np.arange(tm_proj)[None, :], BF16)
    tri_later = jnp.asarray(np.arange(tq)[:, None] > np.arange(tq)[None, :], BF16)
    cos_t, sin_t = _rope_tables(s)
    place, place_f, q_const, k_ones, bound_lane = _fox_placement()

    gains = jnp.tile(qk_gain, (1, 1, HEADS_PER_BRANCH))
    km_all, vm_all = _mem_kv(mem, mem_ln_gain[:, None, :], w_mem_kv.astype(BF16), gains[:, 5:6, :], group_mean)

    assert w_in.shape[2] == (_N_SEC + N_BRANCH * D_MODEL // bw) * bw + HEADS_PER_BRANCH, w_in.shape
    main_chunks = _N_SEC * bw // PREP_CHUNK
    w_main_all = _prep_weights(w_in, 0, main_chunks)
    w_gate_all = _prep_weights(w_in, main_chunks, N_BRANCH * D_MODEL // PREP_CHUNK)
    wf_all = jnp.pad(w_in[:, :, _FORGET_COL:_FORGET_COL + HEADS_PER_BRANCH],
                     ((0, 0), (0, 0), (0, V7X_LANES - HEADS_PER_BRANCH))).astype(BF16)

    for l in range(depth):
        w_main, w_gate, wf = w_main_all[l], w_gate_all[l], wf_all[l]
        bf = jnp.pad(b_forget[l], (0, V7X_LANES - HEADS_PER_BRANCH))[None, :]
        gains_l = jnp.pad(gains[l], ((0, 2), (0, 0)))
        ln_g = ln_gain[l][None, :]

        qk_bound = (FOX_BOUND_MARGIN * HEAD_DIM ** 0.5
                    * jnp.max(jnp.abs(qk_gain[l, 0])) * jnp.max(jnp.abs(qk_gain[l, 1])))
        k_const = k_ones + qk_bound * bound_lane

        outs = _proj(x, ln_g, w_main, wf, bf, gains_l, cos_t, sin_t, group_mean, tri_prefix, place, place_f,
                     q_const, k_const, tm_proj)
        q_sb, k_sb, v_sb, q_fx, k_fx, vt_fx = outs[:6]
        dil_qkv = outs[6:6 + 3 * N_DIL]
        q_mm, z = outs[6 + 3 * N_DIL:]

        o_sb = _sb_attention(q_sb, k_sb, v_sb, tri_later, tq)
        o_fx = lax.cond(qk_bound <= FOX_SAFE_BOUND,
                        functools.partial(_fox_attention, bounded=True),
                        functools.partial(_fox_attention, bounded=False), q_fx, k_fx, vt_fx)
        dil = [_dil_attention(*dil_qkv[3 * g:3 * g + 3], window, dilation)
               for g, (window, dilation) in enumerate(DIL_PATTERNS)]
        w_br = jnp.stack([w_br_sb[l], w_br_fox[l], w_br_dil[l], w_br_mem[l]]).astype(BF16)
        x = _out_layer(x, ln_g, o_sb, o_fx, dil, q_mm, km_all[l], vm_all[l], z, w_gate, w_br,
                       w_out[l].astype(BF16), tm_out)
    return x
```

```python
import functools

import jax
import jax.numpy as jnp
import numpy as np
from jax import lax
from jax.experimental import pallas as pl
from jax.experimental.pallas import tpu as pltpu

F32 = jnp.float32
BF16 = jnp.bfloat16

D_MODEL = 1024
HEAD_DIM = 64
HEADS_PER_BRANCH = 4
BRANCH_WIDTH = HEADS_PER_BRANCH * HEAD_DIM
DIL_PATTERNS = ((128, 1), (512, 4), (2048, 16))
N_DIL = len(DIL_PATTERNS)
N_BRANCH = 4
DIL_BLOCK = 128
DIL_SPAN = 2048
DIL_BLOCKS_PER_GROUP = 4
ROPE_THETA = 10000.0
EPS = 1e-6
MASKED_LOGIT = -1e4
QK_SCALE = HEAD_DIM ** -0.5
SB_UNDERFLOW = -105.0
V7X_VMEM_LIMIT_BYTES = 56 * 1024 * 1024
V7X_LANES = 128
V7X_BF16_SUBLANES = 16
FOX_HEAD_LANES = V7X_LANES
FOX_BIAS_TERMS = 3
FOX_LANE_KEY_F = HEAD_DIM
FOX_LANE_QRY_F = HEAD_DIM + FOX_BIAS_TERMS
FOX_LANE_BOUND = HEAD_DIM + 2 * FOX_BIAS_TERMS
FOX_BOUND_MARGIN = 1.02
FOX_SAFE_BOUND = 30.0
FOX_VT_ROWS = HEAD_DIM + V7X_BF16_SUBLANES
FOX_BLOCK = 512
FOX_BLOCKS_PER_TRIP = 4

_NT = (((1,), (1,)), ((), ()))


def _dot(a, b):
    return jnp.dot(a, b, preferred_element_type=F32)


def _dot_nt(a, b):
    return lax.dot_general(a, b, _NT, preferred_element_type=F32)


def _bf16_terms(x, terms):
    out = []
    r = x
    for t in range(terms):
        p = r.astype(BF16)
        out.append(p)
        if t + 1 < terms:
            r = r - p.astype(F32)
    return out


def _rms_rows(x, gain):
    ms = jnp.mean(x * x, axis=-1, keepdims=True)
    return x * lax.rsqrt(ms + EPS) * gain


def _head_rms(x, group_mean, gain):
    ms = sum(_dot(p, group_mean) for p in _bf16_terms(x * x, 2))
    return x * lax.rsqrt(ms + EPS) * gain


def _sigmoid(x):
    return 1.0 / (1.0 + jnp.exp(-x))


def _log_sigmoid(x):
    return jnp.minimum(x, 0.0) - jnp.log(1.0 + jnp.exp(-jnp.abs(x)))


def _rope(x, cos, sin_signed):
    lane = lax.broadcasted_iota(jnp.int32, x.shape, 1)
    first_half = (lane & (HEAD_DIM - 1)) < (HEAD_DIM // 2)
    w = x.shape[1]
    partner = jnp.where(first_half, pltpu.roll(x, w - HEAD_DIM // 2, 1), pltpu.roll(x, HEAD_DIM // 2, 1))
    return x * cos + partner * sin_signed


def _mem_kv_kernel(mem_ref, g_ref, w_ref, gk_ref, gm_ref, km_ref, vm_ref):
    h = _rms_rows(mem_ref[0], g_ref[0]).astype(BF16)
    kv = _dot(h, w_ref[0])
    km_ref[0, 0] = _head_rms(kv[:, :BRANCH_WIDTH], gm_ref[...], gk_ref[0]).astype(BF16)
    vm_ref[0, 0] = kv[:, BRANCH_WIDTH:].astype(BF16)


def _mem_kv(mem, mem_ln_gain, w_mem_kv, gain_k, group_mean):
    depth = w_mem_kv.shape[0]
    b, n_mem, _ = mem.shape
    out = jax.ShapeDtypeStruct((depth, b, n_mem, BRANCH_WIDTH), BF16)
    return pl.pallas_call(
        _mem_kv_kernel,
        grid=(depth, b),
        in_specs=[
            pl.BlockSpec((1, n_mem, D_MODEL), lambda l, i: (i, 0, 0)),
            pl.BlockSpec((1, 1, D_MODEL), lambda l, i: (l, 0, 0)),
            pl.BlockSpec((1, D_MODEL, 2 * BRANCH_WIDTH), lambda l, i: (l, 0, 0)),
            pl.BlockSpec((1, 1, BRANCH_WIDTH), lambda l, i: (l, 0, 0)),
            pl.BlockSpec((BRANCH_WIDTH, BRANCH_WIDTH), lambda l, i: (0, 0)),
        ],
        out_specs=[pl.BlockSpec((1, 1, n_mem, BRANCH_WIDTH), lambda l, i: (l, i, 0, 0))] * 2,
        out_shape=[out, out],
        compiler_params=pltpu.CompilerParams(dimension_semantics=("arbitrary", "arbitrary")),
        name="mem_kv",
    )(mem, mem_ln_gain, w_mem_kv, gain_k, group_mean)


_SEC_Q_SB, _SEC_K_SB, _SEC_V_SB, _SEC_Z_SB = 0, 1, 2, 3
_SEC_Q_FX, _SEC_K_FX, _SEC_V_FX, _SEC_Z_FX = 4, 5, 6, 7
_SEC_Q_DL, _SEC_K_DL, _SEC_V_DL, _SEC_Z_DL = 8, 11, 14, 17
_SEC_Q_MM, _SEC_Z_MM = 18, 19
_SEC_Z = (_SEC_Z_SB, _SEC_Z_FX, _SEC_Z_DL, _SEC_Z_MM)
_N_SEC = 20
_FORGET_COL = 8 * BRANCH_WIDTH
PREP_CHUNK = 512


def _prep_kernel(a_ref, b_ref, o_ref, *, first_chunk):
    shifted = jnp.concatenate([a_ref[0], b_ref[0]], axis=1)[:, HEADS_PER_BRANCH:HEADS_PER_BRANCH + PREP_CHUNK]
    if first_chunk * PREP_CHUNK >= _FORGET_COL:
        o_ref[0] = shifted.astype(BF16)
    else:
        before_forget = (pl.program_id(1) + first_chunk) * PREP_CHUNK < _FORGET_COL
        o_ref[0] = jnp.where(before_forget, a_ref[0], shifted).astype(BF16)


def _prep_weights(w_in, first_chunk, n_chunks):
    depth, d_model, _ = w_in.shape
    tail = PREP_CHUNK // V7X_LANES
    return pl.pallas_call(
        functools.partial(_prep_kernel, first_chunk=first_chunk),
        grid=(depth, n_chunks),
        in_specs=[pl.BlockSpec((1, d_model, PREP_CHUNK), lambda l, c: (l, 0, c + first_chunk)),
                  pl.BlockSpec((1, d_model, V7X_LANES), lambda l, c: (l, 0, tail * (c + first_chunk + 1)))],
        out_specs=pl.BlockSpec((1, d_model, PREP_CHUNK), lambda l, c: (l, 0, c)),
        out_shape=jax.ShapeDtypeStruct((depth, d_model, n_chunks * PREP_CHUNK), BF16),
        compiler_params=pltpu.CompilerParams(dimension_semantics=("arbitrary", "arbitrary")),
        name="prep_weights",
    )(w_in, w_in)


def _proj_kernel(x_ref, g_ref, w_ref, wf_ref, bf_ref, gains_ref, cos_ref, sin_ref, gm_ref, tri_ref,
                 place_ref, place_f_ref, q_const_ref, k_const_ref,
                 q_sb_ref, k_sb_ref, v_sb_ref, q_fx_ref, k_fx_ref, vt_fx_ref, *rest):
    dil_refs = rest[:3 * N_DIL]
    q_mm_ref, z_ref, stride_ref, carry_ref = rest[3 * N_DIL:]
    tm = x_ref.shape[1]
    h = _rms_rows(x_ref[0], g_ref[...]).astype(BF16)
    gm = gm_ref[...]

    def sec(i):
        return _dot(h, w_ref[:, i * BRANCH_WIDTH:(i + 1) * BRANCH_WIDTH])

    def gain(i):
        return gains_ref[i:i + 1, :]

    @pl.when(pl.program_id(1) == 0)
    def _():
        carry_ref[...] = jnp.zeros_like(carry_ref)

    log_f = _log_sigmoid(_dot(h, wf_ref[...]) + bf_ref[...])

    q_sb_ref[0] = (sec(_SEC_Q_SB) * QK_SCALE).astype(BF16)
    k_sb_ref[0] = sec(_SEC_K_SB).astype(BF16)
    v_sb_ref[0] = sec(_SEC_V_SB).astype(BF16)

    fcum = sum(_dot(tri_ref[...], p) for p in _bf16_terms(log_f, 3)) + carry_ref[...]
    carry_ref[...] = fcum[tm - 1:tm, :]

    for c in range(N_BRANCH):
        z = sec(_SEC_Z[c])
        z_ref[0, :, c * BRANCH_WIDTH:(c + 1) * BRANCH_WIDTH] = z * _sigmoid(z)

    place = place_ref[...]
    q_fx = (_head_rms(sec(_SEC_Q_FX), gm, gain(0)) * QK_SCALE).astype(BF16)
    k_fx = _head_rms(sec(_SEC_K_FX), gm, gain(1)).astype(BF16)
    f_lanes = sum(_dot(p, place_f_ref[t]) for t, p in enumerate(_bf16_terms(fcum, FOX_BIAS_TERMS)))
    lane = lax.broadcasted_iota(jnp.int32, f_lanes.shape, 1) & (FOX_HEAD_LANES - 1)
    key_f = (lane >= FOX_LANE_KEY_F) & (lane < FOX_LANE_QRY_F)
    q_fx_ref[0] = (_dot(q_fx, place) + jnp.where(key_f, 0.0, f_lanes) + q_const_ref[...]).astype(BF16)
    k_fx_ref[0] = (_dot(k_fx, place) + jnp.where(key_f, f_lanes, 0.0) + k_const_ref[...]).astype(BF16)
    vt = sec(_SEC_V_FX).T.astype(BF16)
    ones_rows = (lax.broadcasted_iota(jnp.int32, (V7X_BF16_SUBLANES, FOX_BLOCK), 0) == 0).astype(BF16)
    for blk in range(tm // FOX_BLOCK):
        for hd in range(HEADS_PER_BRANCH):
            r0 = hd * FOX_VT_ROWS
            vt_fx_ref[0, blk, r0:r0 + HEAD_DIM, :] = vt[hd * HEAD_DIM:(hd + 1) * HEAD_DIM,
                                                        blk * FOX_BLOCK:(blk + 1) * FOX_BLOCK]
            vt_fx_ref[0, blk, r0 + HEAD_DIM:r0 + FOX_VT_ROWS, :] = ones_rows

    cos = cos_ref[...]
    sin = sin_ref[...]
    slot = 0
    for c, (_, dilation) in enumerate(DIL_PATTERNS):
        q = _rope(_head_rms(sec(_SEC_Q_DL + c), gm, gain(2)), cos, sin) * QK_SCALE
        k = _rope(_head_rms(sec(_SEC_K_DL + c), gm, gain(3)), cos, sin)
        v = sec(_SEC_V_DL + c)
        for t, val in enumerate((q, k, v)):
            out_ref = dil_refs[3 * c + t]
            if dilation == 1:
                out_ref[0, 0] = val.astype(BF16)
            else:
                for half in range(BRANCH_WIDTH // V7X_LANES):
                    stride_ref[slot, half] = val[:, half * V7X_LANES:(half + 1) * V7X_LANES]
                for r in range(dilation):
                    rows = pl.ds(r, tm // dilation, stride=dilation)
                    out_ref[0, r] = jnp.concatenate([stride_ref[slot, half, rows, :]
                                                     for half in range(BRANCH_WIDTH // V7X_LANES)], axis=1).astype(BF16)
                slot += 1

    q_mm_ref[0] = (_head_rms(sec(_SEC_Q_MM), gm, gain(4)) * QK_SCALE).astype(BF16)


def _proj(x, ln_g, w_main, wf, bf, gains, cos_t, sin_t, group_mean, tri, place, place_f, q_const, k_const, tm):
    b, s, _ = x.shape
    bw = BRANCH_WIDTH
    fox_w = HEADS_PER_BRANCH * FOX_HEAD_LANES
    vt_rows = HEADS_PER_BRANCH * FOX_VT_ROWS

    def tok(width, dtype):
        return jax.ShapeDtypeStruct((b, s, width), dtype)

    def tok_spec(width):
        return pl.BlockSpec((1, tm, width), lambda i, j: (i, j, 0))

    def const_spec(shape):
        return pl.BlockSpec(shape, lambda i, j: (0,) * len(shape))

    dil_shapes, dil_specs = [], []
    for _, d in DIL_PATTERNS:
        dil_shapes += [jax.ShapeDtypeStruct((b, d, s // d, bw), BF16)] * 3
        dil_specs += [pl.BlockSpec((1, d, tm // d, bw), lambda i, j: (i, 0, j, 0))] * 3

    out_shape = ([tok(bw, BF16)] * 3 + [tok(fox_w, BF16)] * 2
                 + [jax.ShapeDtypeStruct((b, s // FOX_BLOCK, vt_rows, FOX_BLOCK), BF16)]
                 + dil_shapes + [tok(bw, BF16), tok(N_BRANCH * bw, F32)])
    out_specs = ([tok_spec(bw)] * 3 + [tok_spec(fox_w)] * 2
                 + [pl.BlockSpec((1, tm // FOX_BLOCK, vt_rows, FOX_BLOCK), lambda i, j: (i, j, 0, 0))]
                 + dil_specs + [tok_spec(bw), tok_spec(N_BRANCH * bw)])
    return pl.pallas_call(
        _proj_kernel,
        grid=(b, s // tm),
        in_specs=[
            tok_spec(D_MODEL),
            const_spec((1, D_MODEL)),
            const_spec((D_MODEL, _N_SEC * bw)),
            const_spec((D_MODEL, V7X_LANES)),
            const_spec((1, V7X_LANES)),
            const_spec((8, bw)),
            pl.BlockSpec((tm, bw), lambda i, j: (j, 0)),
            pl.BlockSpec((tm, bw), lambda i, j: (j, 0)),
            const_spec((bw, bw)),
            const_spec((tm, tm)),
            const_spec((bw, fox_w)),
            const_spec((FOX_BIAS_TERMS, V7X_LANES, fox_w)),
            const_spec((1, fox_w)),
            const_spec((1, fox_w)),
        ],
        out_specs=out_specs,
        out_shape=out_shape,
        scratch_shapes=[pltpu.VMEM((3 * sum(d > 1 for _, d in DIL_PATTERNS), bw // V7X_LANES, tm, V7X_LANES), F32),
                        pltpu.VMEM((1, V7X_LANES), F32)],
        compiler_params=pltpu.CompilerParams(dimension_semantics=("arbitrary", "arbitrary"),
                                             vmem_limit_bytes=V7X_VMEM_LIMIT_BYTES),
        name="proj",
    )(x, ln_g, w_main, wf, bf, gains, cos_t, sin_t, group_mean, tri, place, place_f, q_const, k_const)


def _sb_kernel(q_ref, k_ref, v_ref, tri_ref, o_ref, acc_ref, c_ref, *, tq):
    i = pl.program_id(1)
    q = q_ref[0]
    tri = tri_ref[...]
    acc_ref[...] = jnp.zeros_like(acc_ref)
    c_ref[...] = jnp.zeros_like(c_ref)
    heads = range(HEADS_PER_BRANCH)
    cols = [slice(hd * HEAD_DIM, (hd + 1) * HEAD_DIM) for hd in heads]

    def block(j, diagonal):
        k0 = pl.multiple_of(j * tq, tq)
        kb = k_ref[0, pl.ds(k0, tq), :]
        vb = v_ref[0, pl.ds(k0, tq), :]
        z = [_dot_nt(q[:, cols[hd]], kb[:, cols[hd]]) for hd in heads]
        if diagonal:
            row = lax.broadcasted_iota(jnp.int32, (tq, tq), 0)
            col = lax.broadcasted_iota(jnp.int32, (tq, tq), 1)
            z = [jnp.where(col < row, z[hd], MASKED_LOGIT) for hd in heads]
        t = [jnp.log(1.0 + jnp.exp(-jnp.abs(z[hd]))) for hd in heads]
        log_beta = [jnp.minimum(z[hd], 0.0) - t[hd] for hd in heads]
        log_keep = [log_beta[hd] - z[hd] for hd in heads]
        c = [c_ref[hd] for hd in heads]
        later = [_dot(log_keep[hd].astype(BF16), tri) + c[hd] for hd in heads]
        a = [jnp.exp(log_beta[hd] + later[hd]).astype(BF16) for hd in heads]
        pv = [_dot(a[hd], vb[:, cols[hd]]) for hd in heads]
        c_max = jnp.float32(-jnp.inf)
        for hd in heads:
            acc_ref[:, cols[hd]] += pv[hd]
            c_new = c[hd] + jnp.sum(log_keep[hd], axis=-1, keepdims=True)
            c_ref[hd] = c_new
            c_max = jnp.maximum(c_max, jnp.max(c_new))
        return c_max

    def cond(state):
        j, c_max = state
        return jnp.logical_and(j >= 0, c_max > SB_UNDERFLOW)

    def body(state):
        j, _ = state
        return j - 1, block(j, False)

    lax.while_loop(cond, body, (i - 1, block(i, True)))
    o_ref[0] = acc_ref[...]


def _sb_attention(q, k, v, tri_strict, tq):
    b, s, w = q.shape
    return pl.pallas_call(
        functools.partial(_sb_kernel, tq=tq),
        grid=(b, s // tq),
        in_specs=[
            pl.BlockSpec((1, tq, w), lambda i, j: (i, j, 0)),
            pl.BlockSpec((1, s, w), lambda i, j: (i, 0, 0)),
            pl.BlockSpec((1, s, w), lambda i, j: (i, 0, 0)),
            pl.BlockSpec((tq, tq), lambda i, j: (0, 0)),
        ],
        out_specs=pl.BlockSpec((1, tq, w), lambda i, j: (i, j, 0)),
        out_shape=jax.ShapeDtypeStruct((b, s, w), F32),
        scratch_shapes=[pltpu.VMEM((tq, w), F32), pltpu.VMEM((HEADS_PER_BRANCH, tq, 1), F32)],
        compiler_params=pltpu.CompilerParams(dimension_semantics=("arbitrary", "arbitrary"),
                                             vmem_limit_bytes=V7X_VMEM_LIMIT_BYTES),
        name="sb_attention",
    )(q, k, v, tri_strict)


def _fox_scores(k_ref, q, j, masked):
    k0 = pl.multiple_of(j * FOX_BLOCK, FOX_BLOCK)
    kb = k_ref[0, pl.ds(k0, FOX_BLOCK), :]
    scores = []
    for hd in range(HEADS_PER_BRANCH):
        lanes = slice(hd * FOX_HEAD_LANES, (hd + 1) * FOX_HEAD_LANES)
        s = _dot_nt(kb[:, lanes], q[:, lanes])
        if masked:
            key = lax.broadcasted_iota(jnp.int32, s.shape, 0)
            qry = lax.broadcasted_iota(jnp.int32, s.shape, 1)
            s = jnp.where(key <= qry, s, -jnp.inf)
        scores.append(s)
    return scores


def _fox_values(vt_ref, j, hd):
    return vt_ref[0, j, hd * FOX_VT_ROWS:(hd + 1) * FOX_VT_ROWS, :]


def _fox_finish(acc_ref, o_ref):
    for hd in range(HEADS_PER_BRANCH):
        acc = acc_ref[hd]
        o_t = acc[:HEAD_DIM] / acc[HEAD_DIM:HEAD_DIM + 1]
        o_ref[0, :, hd * HEAD_DIM:(hd + 1) * HEAD_DIM] = o_t.T


def _fox_bounded_kernel(q_ref, k_ref, vt_ref, o_ref, acc_ref):
    i = pl.program_id(1)
    q = q_ref[0]
    heads = range(HEADS_PER_BRANCH)
    acc_ref[...] = jnp.zeros_like(acc_ref)

    def step(blocks, masked):
        scores = [_fox_scores(k_ref, q, j, masked) for j in blocks]
        probs = [[jnp.exp(s[hd]).astype(BF16) for hd in heads] for s in scores]
        pv = [[_dot(_fox_values(vt_ref, j, hd), p[hd]) for hd in heads] for j, p in zip(blocks, probs)]
        for hd in heads:
            acc_ref[hd] += sum(block_pv[hd] for block_pv in pv)

    def four_blocks(g, carry):
        step(tuple(FOX_BLOCKS_PER_TRIP * g + t for t in range(FOX_BLOCKS_PER_TRIP)), False)
        return carry

    lax.fori_loop(0, i // FOX_BLOCKS_PER_TRIP, four_blocks, 0)
    rest = i % FOX_BLOCKS_PER_TRIP
    for n in range(1, FOX_BLOCKS_PER_TRIP):
        @pl.when(rest == n)
        def _(n=n):
            step(tuple(i - n + t for t in range(n)), False)

    step((i,), True)
    _fox_finish(acc_ref, o_ref)


def _fox_online_kernel(q_ref, k_ref, vt_ref, o_ref, acc_ref, m_ref):
    i = pl.program_id(1)
    q = q_ref[0]
    heads = range(HEADS_PER_BRANCH)
    acc_ref[...] = jnp.zeros_like(acc_ref)
    m_ref[...] = jnp.full_like(m_ref, -jnp.inf)

    def step(j, masked):
        scores = _fox_scores(k_ref, q, j, masked)
        m_old = [m_ref[hd] for hd in heads]
        m_new = [jnp.maximum(m_old[hd], jnp.max(scores[hd], axis=0, keepdims=True)) for hd in heads]
        probs = [jnp.exp(scores[hd] - m_new[hd]).astype(BF16) for hd in heads]
        pv = [_dot(_fox_values(vt_ref, j, hd), probs[hd]) for hd in heads]
        for hd in heads:
            acc_ref[hd] = jnp.exp(m_old[hd] - m_new[hd]) * acc_ref[hd] + pv[hd]
            m_ref[hd] = m_new[hd]

    def body(j, carry):
        step(j, False)
        return carry

    lax.fori_loop(0, i, body, 0)
    step(i, True)
    _fox_finish(acc_ref, o_ref)


def _fox_attention(q, k, vt, bounded):
    b, s, w = q.shape
    tq = FOX_BLOCK
    n_blk, vt_rows, _ = vt.shape[1:]
    scratch = [pltpu.VMEM((HEADS_PER_BRANCH, FOX_VT_ROWS, tq), F32)]
    if not bounded:
        scratch.append(pltpu.VMEM((HEADS_PER_BRANCH, 1, tq), F32))
    return pl.pallas_call(
        _fox_bounded_kernel if bounded else _fox_online_kernel,
        grid=(b, s // tq),
        in_specs=[
            pl.BlockSpec((1, tq, w), lambda i, j: (i, j, 0)),
            pl.BlockSpec((1, s, w), lambda i, j: (i, 0, 0)),
            pl.BlockSpec((1, n_blk, vt_rows, FOX_BLOCK), lambda i, j: (i, 0, 0, 0)),
        ],
        out_specs=pl.BlockSpec((1, tq, BRANCH_WIDTH), lambda i, j: (i, j, 0)),
        out_shape=jax.ShapeDtypeStruct((b, s, BRANCH_WIDTH), F32),
        scratch_shapes=scratch,
        compiler_params=pltpu.CompilerParams(dimension_semantics=("arbitrary", "arbitrary"),
                                             vmem_limit_bytes=V7X_VMEM_LIMIT_BYTES),
        name="fox_attention_bounded" if bounded else "fox_attention_online",
    )(q, k, vt)


def _dil_kernel(q_ref, kc_ref, kp_ref, vc_ref, vp_ref, o_ref, lse_ref, stride_ref, *, n_back, dilation):
    span_u = q_ref.shape[2]
    n_blk = span_u // DIL_BLOCK
    heads = range(HEADS_PER_BRANCH)
    heads_per_tile = V7X_LANES // HEAD_DIM
    n_tiles = BRANCH_WIDTH // V7X_LANES
    key = lax.broadcasted_iota(jnp.int32, (2 * DIL_BLOCK, DIL_BLOCK), 0)
    qry = lax.broadcasted_iota(jnp.int32, (2 * DIL_BLOCK, DIL_BLOCK), 1)
    dist = qry + DIL_BLOCK - key
    in_window = (dist >= 0) & (dist <= n_back)
    first_key = jnp.where(pl.program_id(1) == 0, DIL_BLOCK, 0)
    in_window_first = in_window & (key >= first_key)
    lane = lax.broadcasted_iota(jnp.int32, (DIL_BLOCK, V7X_LANES), 1)
    head_lanes = [(lane >= i * HEAD_DIM) & (lane < (i + 1) * HEAD_DIM) for i in range(heads_per_tile)]

    def transposed(v):
        return v.astype(F32).T.astype(BF16)

    vt_cache = {}

    def values_t(r, ub):
        if (r, ub) not in vt_cache:
            src = (vp_ref[0, r, span_u - DIL_BLOCK:, :] if ub < 0
                   else vc_ref[0, r, ub * DIL_BLOCK:(ub + 1) * DIL_BLOCK, :])
            vt_cache[r, ub] = transposed(src)
        return vt_cache[r, ub]

    dims = [slice(hd * HEAD_DIM, (hd + 1) * HEAD_DIM) for hd in heads]
    blocks = [(r, ub) for r in range(dilation) for ub in range(n_blk)]
    for g0 in range(0, len(blocks), DIL_BLOCKS_PER_GROUP):
        group = blocks[g0:g0 + DIL_BLOCKS_PER_GROUP]
        chains = [(r, ub, hd) for r, ub in group for hd in heads]
        s = {}
        for r, ub in group:
            rows = slice(ub * DIL_BLOCK, (ub + 1) * DIL_BLOCK)
            q = q_ref[0, r, rows, :]
            k_prev = (kp_ref[0, r, span_u - DIL_BLOCK:, :] if ub == 0
                      else kc_ref[0, r, (ub - 1) * DIL_BLOCK:ub * DIL_BLOCK, :])
            kcat = jnp.concatenate([k_prev, kc_ref[0, r, rows, :]], axis=0)
            valid = in_window_first if ub == 0 else in_window
            for hd in heads:
                tile = slice(hd // heads_per_tile * V7X_LANES, (hd // heads_per_tile + 1) * V7X_LANES)
                q_hd = jnp.where(head_lanes[hd % heads_per_tile], q[:, tile], jnp.zeros_like(q[:, tile]))
                s[r, ub, hd] = jnp.where(valid, _dot_nt(kcat[:, tile], q_hd), -jnp.inf)
        m = {ch: jnp.max(s[ch], axis=0, keepdims=True) for ch in chains}
        p = {ch: jnp.exp(s[ch] - m[ch]) for ch in chains}
        denom = {ch: jnp.sum(p[ch], axis=0, keepdims=True) for ch in chains}
        pb = {ch: p[ch].astype(BF16) for ch in chains}
        pv = {(r, ub, hd): (_dot(values_t(r, ub - 1)[dims[hd], :], pb[r, ub, hd][:DIL_BLOCK])
                            + _dot(values_t(r, ub)[dims[hd], :], pb[r, ub, hd][DIL_BLOCK:]))
              for r, ub, hd in chains}
        for r, ub in group:
            outs = [pv[r, ub, hd] / denom[r, ub, hd] for hd in heads]
            lses = [jnp.broadcast_to(m[r, ub, hd] + jnp.log(denom[r, ub, hd]), (HEAD_DIM, DIL_BLOCK)) for hd in heads]
            for t, (vals, out_ref) in enumerate(((outs, o_ref), (lses, lse_ref))):
                for tile in range(n_tiles):
                    pair = vals[tile * heads_per_tile:(tile + 1) * heads_per_tile]
                    val = jnp.concatenate(pair, axis=0).T
                    if dilation == 1:
                        out_ref[0, ub * DIL_BLOCK:(ub + 1) * DIL_BLOCK, tile * V7X_LANES:(tile + 1) * V7X_LANES] = val
                    else:
                        dst = pl.ds(ub * DIL_BLOCK * dilation + r, DIL_BLOCK, stride=dilation)
                        stride_ref[t, tile, dst, :] = val
    if dilation > 1:
        for t, out_ref in enumerate((o_ref, lse_ref)):
            for tile in range(n_tiles):
                out_ref[0, :, tile * V7X_LANES:(tile + 1) * V7X_LANES] = stride_ref[t, tile]


def _dil_attention(q, k, v, window, dilation):
    b, _, l, w = q.shape
    s = l * dilation
    span = min(DIL_SPAN, s)
    assert s % span == 0 and (span // dilation) % DIL_BLOCK == 0, (s, span, dilation)
    blk = (1, dilation, span // dilation, w)

    def cur(i, j):
        return (i, 0, j, 0)

    def prev(i, j):
        return (i, 0, jnp.maximum(j - 1, 0), 0)

    out = jax.ShapeDtypeStruct((b, s, w), F32)
    return pl.pallas_call(
        functools.partial(_dil_kernel, n_back=window // dilation, dilation=dilation),
        grid=(b, s // span),
        in_specs=[pl.BlockSpec(blk, cur), pl.BlockSpec(blk, cur), pl.BlockSpec(blk, prev),
                  pl.BlockSpec(blk, cur), pl.BlockSpec(blk, prev)],
        out_specs=[pl.BlockSpec((1, span, w), lambda i, j: (i, j, 0))] * 2,
        out_shape=[out, out],
        scratch_shapes=[pltpu.VMEM((2, w // V7X_LANES, span, V7X_LANES), F32)],
        compiler_params=pltpu.CompilerParams(dimension_semantics=("arbitrary", "arbitrary"),
                                             vmem_limit_bytes=V7X_VMEM_LIMIT_BYTES),
        name=f"dil_attention_d{dilation}",
    )(q, k, k, v, v)


def _out_kernel(x_ref, g_ref, o_sb_ref, o_fx_ref, o0_ref, l0_ref, o1_ref, l1_ref, o2_ref, l2_ref,
                q_mm_ref, km_ref, vm_ref, z_ref, wg_ref, wbr_ref, wout_ref, y_ref):
    x = x_ref[0]
    h = _rms_rows(x, g_ref[...]).astype(BF16)

    l0, l1, l2 = l0_ref[0], l1_ref[0], l2_ref[0]
    m = jnp.maximum(jnp.maximum(l0, l1), l2)
    e0, e1, e2 = jnp.exp(l0 - m), jnp.exp(l1 - m), jnp.exp(l2 - m)
    o_dl = (e0 * o0_ref[0] + e1 * o1_ref[0] + e2 * o2_ref[0]) / (e0 + e1 + e2)

    q = q_mm_ref[0]
    km = km_ref[0]
    vm = vm_ref[0]
    heads = range(HEADS_PER_BRANCH)
    cols = [slice(hd * HEAD_DIM, (hd + 1) * HEAD_DIM) for hd in heads]
    s = [_dot_nt(q[:, cols[hd]], km[:, cols[hd]]) for hd in heads]
    p = [jnp.exp(s[hd] - jnp.max(s[hd], axis=-1, keepdims=True)) for hd in heads]
    pv = [_dot(p[hd].astype(BF16), vm[:, cols[hd]]) for hd in heads]
    o_mm = jnp.concatenate([pv[hd] / jnp.sum(p[hd], axis=-1, keepdims=True) for hd in heads], axis=1)

    merged = None
    for br, o in enumerate((o_sb_ref[0], o_fx_ref[0], o_dl, o_mm)):
        gated = (o * z_ref[0, :, br * BRANCH_WIDTH:(br + 1) * BRANCH_WIDTH]).astype(BF16)
        y = _dot(gated, wbr_ref[br])
        gate = _sigmoid(_dot(h, wg_ref[:, br * D_MODEL:(br + 1) * D_MODEL]))
        merged = gate * y if merged is None else merged + gate * y
    y_ref[0] = x + _dot(merged.astype(BF16), wout_ref[...])


def _out_layer(x, ln_g, o_sb, o_fx, dil, q_mm, km, vm, z, w_gate, w_br, w_out, tm):
    b, s, _ = x.shape
    bw = BRANCH_WIDTH
    n_mem = km.shape[1]

    def tok_spec(width):
        return pl.BlockSpec((1, tm, width), lambda i, j: (i, j, 0))

    def const_spec(shape):
        return pl.BlockSpec(shape, lambda i, j: (0,) * len(shape), pipeline_mode=pl.Buffered(1))

    mem_spec = pl.BlockSpec((1, n_mem, bw), lambda i, j: (i, 0, 0))
    dil_args = [t for pair in dil for t in pair]
    return pl.pallas_call(
        _out_kernel,
        grid=(b, s // tm),
        in_specs=([tok_spec(D_MODEL), const_spec((1, D_MODEL)), tok_spec(bw), tok_spec(bw)]
                  + [tok_spec(bw)] * (2 * N_DIL)
                  + [tok_spec(bw), mem_spec, mem_spec, tok_spec(N_BRANCH * bw),
                     const_spec((D_MODEL, N_BRANCH * D_MODEL)), const_spec((N_BRANCH, bw, D_MODEL)),
                     const_spec((D_MODEL, D_MODEL))]),
        out_specs=tok_spec(D_MODEL),
        out_shape=jax.ShapeDtypeStruct((b, s, D_MODEL), F32),
        compiler_params=pltpu.CompilerParams(dimension_semantics=("arbitrary", "arbitrary"),
                                             vmem_limit_bytes=V7X_VMEM_LIMIT_BYTES),
        name="out_layer",
    )(x, ln_g, o_sb, o_fx, *dil_args, q_mm, km, vm, z, w_gate, w_br, w_out)


def _tile(n, pref):
    t = min(pref, n)
    assert n % t == 0, (n, t)
    return t


def _rope_tables(s):
    half = HEAD_DIM // 2
    inv = ROPE_THETA ** (-jnp.arange(half, dtype=F32) / half)
    ang = jnp.arange(s, dtype=F32)[:, None] * inv[None, :]
    cos, sin = jnp.cos(ang), jnp.sin(ang)
    cos_t = jnp.tile(jnp.concatenate([cos, cos], axis=1), (1, HEADS_PER_BRANCH))
    sin_t = jnp.tile(jnp.concatenate([-sin, sin], axis=1), (1, HEADS_PER_BRANCH))
    return cos_t, sin_t


def _fox_placement():
    fox_w = HEADS_PER_BRANCH * FOX_HEAD_LANES
    place = np.zeros((BRANCH_WIDTH, fox_w), np.float32)
    place_f = np.zeros((FOX_BIAS_TERMS, V7X_LANES, fox_w), np.float32)
    q_const = np.zeros((1, fox_w), np.float32)
    k_ones = np.zeros((1, fox_w), np.float32)
    bound_lane = np.zeros((1, fox_w), np.float32)
    for hd in range(HEADS_PER_BRANCH):
        base = hd * FOX_HEAD_LANES
        for d in range(HEAD_DIM):
            place[hd * HEAD_DIM + d, base + d] = 1.0
        for t in range(FOX_BIAS_TERMS):
            place_f[t, hd, base + FOX_LANE_KEY_F + t] = 1.0
            place_f[t, hd, base + FOX_LANE_QRY_F + t] = 1.0
            q_const[0, base + FOX_LANE_KEY_F + t] = -1.0
            k_ones[0, base + FOX_LANE_QRY_F + t] = 1.0
        q_const[0, base + FOX_LANE_BOUND] = -1.0
        bound_lane[0, base + FOX_LANE_BOUND] = 1.0
    return (jnp.asarray(place, BF16), jnp.asarray(place_f, BF16), jnp.asarray(q_const), jnp.asarray(k_ones),
            jnp.asarray(bound_lane))


def kernel(x, mem, ln_gain, mem_ln_gain, qk_gain, w_in, b_forget, w_mem_kv, w_br_sb, w_br_fox, w_br_dil,
           w_br_mem, w_out):
    depth = w_in.shape[0]
    b, s, _ = x.shape
    bw = BRANCH_WIDTH
    tm_proj = _tile(s, 512)
    tq = _tile(s, 256)
    tm_out = _tile(s, 512)
    assert tm_proj % FOX_BLOCK == 0 and s % FOX_BLOCK == 0

    head_id = np.arange(bw) // HEAD_DIM
    group_mean = jnp.asarray((head_id[:, None] == head_id[None, :]) / HEAD_DIM, BF16)
    tri_prefix = jnp.asarray(np.arange(tm_proj)[:, None] >= np.arange(tm_proj)[None, :], BF16)
    tri_later = jnp.asarray(np.arange(tq)[:, None] > np.arange(tq)[None, :], BF16)
    cos_t, sin_t = _rope_tables(s)
    place, place_f, q_const, k_ones, bound_lane = _fox_placement()

    gains = jnp.tile(qk_gain, (1, 1, HEADS_PER_BRANCH))
    km_all, vm_all = _mem_kv(mem, mem_ln_gain[:, None, :], w_mem_kv.astype(BF16), gains[:, 5:6, :], group_mean)

    assert w_in.shape[2] == (_N_SEC + N_BRANCH * D_MODEL // bw) * bw + HEADS_PER_BRANCH, w_in.shape
    main_chunks = _N_SEC * bw // PREP_CHUNK
    w_main_all = _prep_weights(w_in, 0, main_chunks)
    w_gate_all = _prep_weights(w_in, main_chunks, N_BRANCH * D_MODEL // PREP_CHUNK)
    wf_all = jnp.pad(w_in[:, :, _FORGET_COL:_FORGET_COL + HEADS_PER_BRANCH],
                     ((0, 0), (0, 0), (0, V7X_LANES - HEADS_PER_BRANCH))).astype(BF16)

    for l in range(depth):
        w_main, w_gate, wf = w_main_all[l], w_gate_all[l], wf_all[l]
        bf = jnp.pad(b_forget[l], (0, V7X_LANES - HEADS_PER_BRANCH))[None, :]
        gains_l = jnp.pad(gains[l], ((0, 2), (0, 0)))
        ln_g = ln_gain[l][None, :]

        qk_bound = (FOX_BOUND_MARGIN * HEAD_DIM ** 0.5
                    * jnp.max(jnp.abs(qk_gain[l, 0])) * jnp.max(jnp.abs(qk_gain[l, 1])))
        k_const = k_ones + qk_bound * bound_lane

        outs = _proj(x, ln_g, w_main, wf, bf, gains_l, cos_t, sin_t, group_mean, tri_prefix, place, place_f,
                     q_const, k_const, tm_proj)
        q_sb, k_sb, v_sb, q_fx, k_fx, vt_fx = outs[:6]
        dil_qkv = outs[6:6 + 3 * N_DIL]
        q_mm, z = outs[6 + 3 * N_DIL:]

        o_sb = _sb_attention(q_sb, k_sb, v_sb, tri_later, tq)
        o_fx = lax.cond(qk_bound <= FOX_SAFE_BOUND,
                        functools.partial(_fox_attention, bounded=True),
                        functools.partial(_fox_attention, bounded=False), q_fx, k_fx, vt_fx)
        dil = [_dil_attention(*dil_qkv[3 * g:3 * g + 3], window, dilation)
               for g, (window, dilation) in enumerate(DIL_PATTERNS)]
        w_br = jnp.stack([w_br_sb[l], w_br_fox[l], w_br_dil[l], w_br_mem[l]]).astype(BF16)
        x = _out_layer(x, ln_g, o_sb, o_fx, dil, q_mm, km_all[l], vm_all[l], z, w_gate, w_br,
                       w_out[l].astype(BF16), tm_out)
    return x
```
